```python
import jax, jax.numpy as jnp
from jax import lax
import numpy as np

D_MODEL = 2048
BATCH = 2
SEQ = 4096
DEPTH = 4
DEC_BATCH = 8
DEC_SEQ = 1
PAST_LEN = 16384
PAGE_SIZE = 128

N_MIXERS = 2
N_HEADS = 16
HEAD_DIM = D_MODEL // N_HEADS
CONV_WIDTH = 31
D_FF = 4 * D_MODEL
Q_BLOCK = 128
N_FOX = (DEPTH + N_MIXERS - 1) // N_MIXERS
N_CONV = DEPTH // N_MIXERS
EPS = 1e-6
NEG_INF = -1e30

kernel_name = 'fox_conformer_hybrid_step'


def rmsnorm(x, g):
    xf = x.astype(jnp.float32)
    y = xf * lax.rsqrt(jnp.mean(xf * xf, axis=-1, keepdims=True) + EPS)
    return (y * g.astype(jnp.float32)).astype(x.dtype)


def layernorm(x, g, b):
    xf = x.astype(jnp.float32)
    mu = jnp.mean(xf, axis=-1, keepdims=True)
    xc = xf - mu
    y = xc * lax.rsqrt(jnp.mean(xc * xc, axis=-1, keepdims=True) + EPS)
    return (y * g.astype(jnp.float32) + b.astype(jnp.float32)).astype(x.dtype)


def adaln(c, w_mod, b_mod):
    m = jax.nn.silu(c) @ w_mod + b_mod
    return m.reshape(c.shape[0], 6, 1, D_MODEL)


def modulate(x, g, shift, scale):
    return rmsnorm(x, g) * (1 + scale) + shift


def sq_relu_mlp(h, w1, w2):
    return jnp.square(jax.nn.relu(h @ w1)) @ w2


def fox_project(h, w_qkvf, b_f, g_q, g_k):
    B, T, _ = h.shape
    z = h @ w_qkvf
    q = rmsnorm(z[..., :D_MODEL].reshape(B, T, N_HEADS, HEAD_DIM), g_q)
    k = rmsnorm(z[..., D_MODEL:2 * D_MODEL].reshape(B, T, N_HEADS, HEAD_DIM), g_k)
    v = z[..., 2 * D_MODEL:3 * D_MODEL].reshape(B, T, N_HEADS, HEAD_DIM)
    logf = jax.nn.log_sigmoid((z[..., 3 * D_MODEL:] + b_f).astype(jnp.float32))
    return q, k, v, logf


def fox_attend(q, cq, qpos, k, v, ck, kpos):
    s = jnp.einsum('bqhd,bkhd->bhqk', q, k).astype(jnp.float32) * (HEAD_DIM ** -0.5)
    s = s + jnp.transpose(cq, (0, 2, 1))[:, :, :, None] - jnp.transpose(ck, (0, 2, 1))[:, :, None, :]
    causal = (kpos[None, :] <= qpos[:, None])[None, None]
    p = jax.nn.softmax(jnp.where(causal, s, NEG_INF), axis=-1).astype(v.dtype)
    return jnp.einsum('bhqk,bkhd->bqhd', p, v)


def fox_prompt_attn(q, k, v, logf):
    B, T = q.shape[:2]
    c = jnp.cumsum(logf, axis=1)
    pos = jnp.arange(T)

    def block(bi):
        start = bi * Q_BLOCK
        qb = lax.dynamic_slice_in_dim(q, start, Q_BLOCK, axis=1)
        cb = lax.dynamic_slice_in_dim(c, start, Q_BLOCK, axis=1)
        return fox_attend(qb, cb, start + jnp.arange(Q_BLOCK), k, v, c, pos)

    o = lax.map(block, jnp.arange(T // Q_BLOCK))
    return jnp.transpose(o, (1, 0, 2, 3, 4)).reshape(B, T, N_HEADS, HEAD_DIM)


def gather_pages(cache, page_table):
    g = cache[page_table]
    return g.reshape((page_table.shape[0], page_table.shape[1] * cache.shape[1]) + cache.shape[2:])


def fox_prompt_mixer(h, w_qkvf, b_f, g_q, g_k, w_o):
    B, T, _ = h.shape
    q, k, v, logf = fox_project(h, w_qkvf, b_f, g_q, g_k)
    o = fox_prompt_attn(q, k, v, logf)
    return o.reshape(B, T, D_MODEL) @ w_o, (k, v, logf)


def fox_sample_mixer(h, k_past, v_past, logf_past, w_qkvf, b_f, g_q, g_k, w_o):
    B, T, _ = h.shape
    q, k, v, logf = fox_project(h, w_qkvf, b_f, g_q, g_k)
    P = k_past.shape[1]
    k_all = jnp.concatenate([k_past.astype(k.dtype), k], axis=1)
    v_all = jnp.concatenate([v_past.astype(v.dtype), v], axis=1)
    c = jnp.cumsum(jnp.concatenate([logf_past.astype(jnp.float32), logf], axis=1), axis=1)
    o = fox_attend(q, c[:, P:], P + jnp.arange(T), k_all, v_all, c, jnp.arange(P + T))
    return o.reshape(B, T, D_MODEL) @ w_o, (k, v, logf)


def conv_mixer(h, hist, w_pw1, b_pw1, w_dw, b_dw, g_ln, b_ln, w_pw2, b_pw2):
    u = h @ w_pw1 + b_pw1
    a, gt = jnp.split(u, 2, axis=-1)
    u = a * jax.nn.sigmoid(gt)
    full = jnp.concatenate([hist.astype(u.dtype), u], axis=1)
    y = lax.conv_general_dilated(full, w_dw[:, None, :].astype(u.dtype), window_strides=(1,), padding='VALID',
                                 dimension_numbers=('NWC', 'WIO', 'NWC'), feature_group_count=D_MODEL) + b_dw
    y = jax.nn.silu(layernorm(y, g_ln, b_ln))
    return y @ w_pw2 + b_pw2, full[:, -(CONV_WIDTH - 1):]


def run_layer(x, c, w_mod_i, b_mod_i, g_mix_i, g_ffn_i, w_ff1_i, w_ff2_i, mixer):
    mod = adaln(c, w_mod_i, b_mod_i)
    out, new_state = mixer(modulate(x, g_mix_i, mod[:, 0], mod[:, 1]))
    x = x + mod[:, 2] * out
    x = x + mod[:, 5] * sq_relu_mlp(modulate(x, g_ffn_i, mod[:, 3], mod[:, 4]), w_ff1_i, w_ff2_i)
    return x, new_state


def setup_inputs(seed: int = 0) -> dict:
    key = jax.random.key(seed)
    ks = jax.random.split(key, 32)
    n_pages = PAST_LEN // PAGE_SIZE
    pool_pages = (DEC_BATCH * n_pages * 5) // 4
    nrm = jax.random.normal
    f32 = jnp.float32
    d_in = D_MODEL ** -0.5
    page_table = jax.random.permutation(ks[8], pool_pages)[:DEC_BATCH * n_pages].reshape(DEC_BATCH, n_pages).astype(jnp.int32)
    return {
        'x_prompt': nrm(ks[0], (BATCH, SEQ, D_MODEL), f32),
        'x_sample': nrm(ks[1], (DEC_BATCH, DEC_SEQ, D_MODEL), f32),
        'cache_k': nrm(ks[2], (N_FOX, pool_pages, PAGE_SIZE, N_HEADS, HEAD_DIM), f32),
        'cache_v': nrm(ks[3], (N_FOX, pool_pages, PAGE_SIZE, N_HEADS, HEAD_DIM), f32),
        'cache_logf': jax.nn.log_sigmoid(2.5 + nrm(ks[4], (N_FOX, pool_pages, PAGE_SIZE, N_HEADS), f32)),
        'state_conv': 0.5 * nrm(ks[5], (N_CONV, DEC_BATCH, CONV_WIDTH - 1, D_MODEL), f32),
        'page_table': page_table,
        'c_prompt': nrm(ks[6], (BATCH, D_MODEL), f32),
        'c_sample': nrm(ks[7], (DEC_BATCH, D_MODEL), f32),
        'w_mod': 0.5 * d_in * nrm(ks[9], (DEPTH, D_MODEL, 6 * D_MODEL), f32),
        'b_mod': 0.02 * nrm(ks[10], (DEPTH, 6 * D_MODEL), f32),
        'g_mix': 1.0 + 0.02 * nrm(ks[11], (DEPTH, D_MODEL), f32),
        'g_ffn': 1.0 + 0.02 * nrm(ks[12], (DEPTH, D_MODEL), f32),
        'w_qkvf': d_in * nrm(ks[13], (N_FOX, D_MODEL, 3 * D_MODEL + N_HEADS), f32),
        'b_f': jax.random.uniform(ks[14], (N_FOX, N_HEADS), f32, 1.0, 4.0),
        'g_q': 1.0 + 0.02 * nrm(ks[15], (N_FOX, HEAD_DIM), f32),
        'g_k': 1.0 + 0.02 * nrm(ks[16], (N_FOX, HEAD_DIM), f32),
        'w_o': d_in * nrm(ks[17], (N_FOX, D_MODEL, D_MODEL), f32),
        'w_pw1': d_in * nrm(ks[18], (N_CONV, D_MODEL, 2 * D_MODEL), f32),
        'b_pw1': 0.02 * nrm(ks[19], (N_CONV, 2 * D_MODEL), f32),
        'w_dw': (CONV_WIDTH ** -0.5) * nrm(ks[20], (N_CONV, CONV_WIDTH, D_MODEL), f32),
        'b_dw': 0.02 * nrm(ks[21], (N_CONV, D_MODEL), f32),
        'g_cln': 1.0 + 0.02 * nrm(ks[22], (N_CONV, D_MODEL), f32),
        'b_cln': 0.02 * nrm(ks[23], (N_CONV, D_MODEL), f32),
        'w_pw2': d_in * nrm(ks[24], (N_CONV, D_MODEL, D_MODEL), f32),
        'b_pw2': 0.02 * nrm(ks[25], (N_CONV, D_MODEL), f32),
        'w_ff1': d_in * nrm(ks[26], (DEPTH, D_MODEL, D_FF), f32),
        'w_ff2': (D_FF ** -0.5) * nrm(ks[27], (DEPTH, D_FF, D_MODEL), f32),
    }


def reference(x_prompt, x_sample, cache_k, cache_v, cache_logf, state_conv, page_table, c_prompt, c_sample,
              w_mod, b_mod, g_mix, g_ffn, w_qkvf, b_f, g_q, g_k, w_o,
              w_pw1, b_pw1, w_dw, b_dw, g_cln, b_cln, w_pw2, b_pw2, w_ff1, w_ff2):
    xp, xs = x_prompt, x_sample
    kp_l, vp_l, lfp_l, cvp_l = [], [], [], []
    ks_l, vs_l, lfs_l, cvs_l = [], [], [], []
    for i in range(DEPTH):
        j = i // N_MIXERS
        if i % N_MIXERS == 0:
            mix_p = lambda h: fox_prompt_mixer(h, w_qkvf[j], b_f[j], g_q[j], g_k[j], w_o[j])
            mix_s = lambda h: fox_sample_mixer(h, gather_pages(cache_k[j], page_table), gather_pages(cache_v[j], page_table),
                                               gather_pages(cache_logf[j], page_table), w_qkvf[j], b_f[j], g_q[j], g_k[j], w_o[j])
        else:
            conv_w = (w_pw1[j], b_pw1[j], w_dw[j], b_dw[j], g_cln[j], b_cln[j], w_pw2[j], b_pw2[j])
            hist0 = jnp.zeros((xp.shape[0], CONV_WIDTH - 1, D_MODEL), xp.dtype)
            mix_p = lambda h: conv_mixer(h, hist0, *conv_w)
            mix_s = lambda h: conv_mixer(h, state_conv[j], *conv_w)
        xp, st_p = run_layer(xp, c_prompt, w_mod[i], b_mod[i], g_mix[i], g_ffn[i], w_ff1[i], w_ff2[i], mix_p)
        xs, st_s = run_layer(xs, c_sample, w_mod[i], b_mod[i], g_mix[i], g_ffn[i], w_ff1[i], w_ff2[i], mix_s)
        if i % N_MIXERS == 0:
            kp_l.append(st_p[0]); vp_l.append(st_p[1]); lfp_l.append(st_p[2])
            ks_l.append(st_s[0]); vs_l.append(st_s[1]); lfs_l.append(st_s[2])
        else:
            cvp_l.append(st_p); cvs_l.append(st_s)
    return (xp, xs, jnp.stack(kp_l), jnp.stack(vp_l), jnp.stack(lfp_l), jnp.stack(cvp_l),
            jnp.stack(ks_l), jnp.stack(vs_l), jnp.stack(lfs_l), jnp.stack(cvs_l))
```

```python
import functools

import jax
import jax.numpy as jnp
from jax import lax
from jax.experimental import pallas as pl
from jax.experimental.pallas import tpu as pltpu

F32 = jnp.float32
BF16 = jnp.bfloat16

N_HEADS = 16
HEAD_DIM = 128
CONV_WIDTH = 31
EPS = 1e-6
NEG_INF = -1e30
LANES = 128
SUBLANES = 8
VMEM_LIMIT = 52 * 1024 * 1024
SAMPLE_ROWS = 16
CONV_HALO = 32
CONV_ROWS = 16


def _params(*sem):
    return pltpu.CompilerParams(dimension_semantics=sem, vmem_limit_bytes=VMEM_LIMIT)


def _dot(a, b):
    return jnp.dot(a, b, preferred_element_type=F32)


def _dot_nt(a, b):
    return lax.dot_general(a, b, (((1,), (1,)), ((), ())), preferred_element_type=F32)


def _split3(x):
    hi = x.astype(BF16)
    r1 = x - hi.astype(F32)
    mid = r1.astype(BF16)
    lo = (r1 - mid.astype(F32)).astype(BF16)
    return hi, mid, lo


def _modnorm(x, g, shift, scale):
    ms = jnp.mean(x * x, axis=-1, keepdims=True)
    y = x * lax.rsqrt(ms + EPS) * g
    return y * (1.0 + scale) + shift


def _log_sigmoid(z):
    return jnp.minimum(z, 0.0) - jnp.log1p(jnp.exp(-jnp.abs(z)))


def _mod_spec(mod, k, d, tiles_per_group):
    r = mod.shape[1]
    return pl.BlockSpec((None, r, d), lambda i, j: (i // tiles_per_group, 0, k))


def _mods_kernel(c_ref, w_ref, b_ref, o_ref):
    c = c_ref[...]
    a = (c * jax.nn.sigmoid(c)).astype(BF16)
    o_ref[...] = _dot(a, w_ref[...]) + b_ref[...]


def _mods(c16, w_mod, b_mod, tn=1024):
    depth, d, n = w_mod.shape
    return pl.pallas_call(
        _mods_kernel,
        out_shape=jax.ShapeDtypeStruct((depth, SAMPLE_ROWS, n), F32),
        grid=(depth, n // tn),
        in_specs=[
            pl.BlockSpec((SAMPLE_ROWS, d), lambda l, j: (0, 0)),
            pl.BlockSpec((None, d, tn), lambda l, j: (l, 0, j)),
            pl.BlockSpec((None, 1, tn), lambda l, j: (l, 0, j)),
        ],
        out_specs=pl.BlockSpec((None, SAMPLE_ROWS, tn), lambda l, j: (l, 0, j)),
        compiler_params=_params("arbitrary", "arbitrary"),
        name="adaln_mods",
    )(c16, w_mod, b_mod.reshape(depth, 1, n))


def _qkvf_kernel(x_ref, g_ref, sh_ref, sc_ref, wq_ref, wk_ref, wv_ref, wf_ref, bf_ref, gq_ref, gk_ref,
                 q_ref, k32_ref, kb_ref, v32_ref, vb_ref, lf_ref, h_ref, *, tn):
    @pl.when(pl.program_id(1) == 0)
    def _():
        hb = _modnorm(x_ref[...], g_ref[...], sh_ref[...], sc_ref[...]).astype(BF16)
        h_ref[...] = hb
        lf_ref[...] = _log_sigmoid(_dot(hb, wf_ref[...]) + bf_ref[...])

    hb = h_ref[...]
    zq = _dot(hb, wq_ref[...])
    zk = _dot(hb, wk_ref[...])
    zv = _dot(hb, wv_ref[...])
    q_scale = HEAD_DIM ** -0.5
    for hh in range(tn // HEAD_DIM):
        sl = slice(hh * HEAD_DIM, (hh + 1) * HEAD_DIM)
        zh = zq[:, sl]
        qn = zh * lax.rsqrt(jnp.mean(zh * zh, axis=-1, keepdims=True) + EPS) * gq_ref[...]
        q_ref[:, sl] = (qn * q_scale).astype(BF16)
        zh = zk[:, sl]
        kn = zh * lax.rsqrt(jnp.mean(zh * zh, axis=-1, keepdims=True) + EPS) * gk_ref[...]
        k32_ref[:, sl] = kn
        kb_ref[:, sl] = kn.astype(BF16)
    v32_ref[...] = zv
    vb_ref[...] = zv.astype(BF16)


def _qkvf(x, g, mod, rows_per_group, w_bf, wf_pad, bf_pad, g_q, g_k, tm, tn=512):
    m, d = x.shape
    nb = d // tn
    tpg = rows_per_group // tm
    row = lambda i, j: (i, j)
    outs = pl.pallas_call(
        functools.partial(_qkvf_kernel, tn=tn),
        out_shape=(
            jax.ShapeDtypeStruct((m, d), BF16),
            jax.ShapeDtypeStruct((m, d), F32),
            jax.ShapeDtypeStruct((m, d), BF16),
            jax.ShapeDtypeStruct((m, d), F32),
            jax.ShapeDtypeStruct((m, d), BF16),
            jax.ShapeDtypeStruct((m, LANES), F32),
        ),
        grid=(m // tm, nb),
        in_specs=[
            pl.BlockSpec((tm, d), lambda i, j: (i, 0)),
            pl.BlockSpec((1, d), lambda i, j: (0, 0)),
            _mod_spec(mod, 0, d, tpg),
            _mod_spec(mod, 1, d, tpg),
            pl.BlockSpec((d, tn), lambda i, j: (0, j)),
            pl.BlockSpec((d, tn), lambda i, j: (0, nb + j)),
            pl.BlockSpec((d, tn), lambda i, j: (0, 2 * nb + j)),
            pl.BlockSpec((d, LANES), lambda i, j: (0, 0)),
            pl.BlockSpec((1, LANES), lambda i, j: (0, 0)),
            pl.BlockSpec((1, HEAD_DIM), lambda i, j: (0, 0)),
            pl.BlockSpec((1, HEAD_DIM), lambda i, j: (0, 0)),
        ],
        out_specs=(
            pl.BlockSpec((tm, tn), row), pl.BlockSpec((tm, tn), row), pl.BlockSpec((tm, tn), row),
            pl.BlockSpec((tm, tn), row), pl.BlockSpec((tm, tn), row),
            pl.BlockSpec((tm, LANES), lambda i, j: (i, 0)),
        ),
        scratch_shapes=[pltpu.VMEM((tm, d), BF16)],
        compiler_params=_params("arbitrary", "arbitrary"),
        name="fox_qkvf",
    )(x, g.reshape(1, d), mod, mod, w_bf, w_bf, w_bf, wf_pad, bf_pad, g_q.reshape(1, -1), g_k.reshape(1, -1))
    return outs


def _cumsum_kernel(x_ref, o_ref):
    n = x_ref.shape[1]
    j = lax.broadcasted_iota(jnp.int32, (LANES, LANES), 0)
    s = lax.broadcasted_iota(jnp.int32, (LANES, LANES), 1)
    tri = (j <= s).astype(BF16)
    carry = jnp.zeros((x_ref.shape[0], 1), F32)
    for c in range(n // LANES):
        sl = slice(c * LANES, (c + 1) * LANES)
        hi, mid, lo = _split3(x_ref[:, sl])
        cs = _dot(hi, tri) + _dot(mid, tri) + _dot(lo, tri)
        o_ref[:, sl] = cs + carry
        carry = carry + cs[:, LANES - 1:LANES]


def _cumsum_rows(x):
    return pl.pallas_call(
        _cumsum_kernel,
        out_shape=jax.ShapeDtypeStruct(x.shape, F32),
        name="logf_cumsum",
    )(x)


def _flash_kernel(q_ref, k_ref, v_ref, c_ref, o_ref, acc_ref, *, tq):
    i = pl.program_id(2)
    q = q_ref[...]
    q0 = pl.multiple_of(i * tq, tq)

    cq_row = c_ref[:, pl.ds(q0, tq)]
    cols = []
    for c in range(tq // LANES):
        blk = jnp.broadcast_to(cq_row[:, c * LANES:(c + 1) * LANES], (LANES, LANES))
        cols.append(blk.T[:, 0:1])
    cq = jnp.concatenate(cols, axis=0)

    acc_ref[...] = jnp.zeros_like(acc_ref)

    def step(k0, m, l, masked):
        kj = k_ref[pl.ds(k0, tq), :]
        vj = v_ref[pl.ds(k0, tq), :]
        s = _dot_nt(q, kj) + cq - c_ref[:, pl.ds(k0, tq)]
        if masked:
            r = lax.broadcasted_iota(jnp.int32, (tq, tq), 0)
            cc = lax.broadcasted_iota(jnp.int32, (tq, tq), 1)
            s = jnp.where(cc <= r, s, NEG_INF)
        m_new = jnp.maximum(m, jnp.max(s, axis=-1, keepdims=True))
        alpha = jnp.exp(m - m_new)
        p = jnp.exp(s - m_new)
        l_new = alpha * l + jnp.sum(p, axis=-1, keepdims=True)
        acc_ref[...] = alpha * acc_ref[...] + _dot(p.astype(BF16), vj)
        return m_new, l_new

    def body(j, carry):
        m, l = carry
        return step(pl.multiple_of(j * tq, tq), m, l, False)

    m0 = jnp.full((tq, 1), NEG_INF, F32)
    l0 = jnp.zeros((tq, 1), F32)
    m, l = lax.fori_loop(0, i, body, (m0, l0))
    m, l = step(q0, m, l, True)
    o_ref[...] = (acc_ref[...] / l).astype(o_ref.dtype)


def _flash(q, k, v, c, tq=512):
    b, t, d = q.shape
    h = d // HEAD_DIM
    return pl.pallas_call(
        functools.partial(_flash_kernel, tq=tq),
        out_shape=jax.ShapeDtypeStruct((b, t, d), BF16),
        grid=(b, h, t // tq),
        in_specs=[
            pl.BlockSpec((None, tq, HEAD_DIM), lambda bi, hi, i: (bi, i, hi)),
            pl.BlockSpec((None, t, HEAD_DIM), lambda bi, hi, i: (bi, 0, hi)),
            pl.BlockSpec((None, t, HEAD_DIM), lambda bi, hi, i: (bi, 0, hi)),
            pl.BlockSpec((None, None, 1, t), lambda bi, hi, i: (bi, hi, 0, 0)),
        ],
        out_specs=pl.BlockSpec((None, tq, HEAD_DIM), lambda bi, hi, i: (bi, i, hi)),
        scratch_shapes=[pltpu.VMEM((tq, HEAD_DIM), F32)],
        compiler_params=_params("arbitrary", "arbitrary", "arbitrary"),
        name="fox_flash",
    )(q, k, v, c)


def _decode_kernel(pt_ref, q_ref, kn_ref, vn_ref, lfn_ref, k_ref, v_ref, lf_ref, o_ref,
                   qbd_ref, acc_ref, m_ref, l_ref, carry_ref):
    p = pl.program_id(1)
    d = acc_ref.shape[1]

    @pl.when(p == 0)
    def _():
        qh = q_ref[...]
        qt = jnp.concatenate([qh] * N_HEADS, axis=1)
        r = lax.broadcasted_iota(jnp.int32, (N_HEADS, d), 0)
        c = lax.broadcasted_iota(jnp.int32, (N_HEADS, d), 1) // HEAD_DIM
        qbd_ref[...] = jnp.where(r == c, qt, jnp.zeros_like(qt))
        kn = kn_ref[...].astype(BF16).astype(F32)
        m_ref[...] = jnp.sum(qh.astype(F32) * kn, axis=-1, keepdims=True)
        l_ref[...] = jnp.ones_like(l_ref)
        acc_ref[...] = jnp.broadcast_to(vn_ref[...].astype(BF16).astype(F32), acc_ref.shape)
        carry_ref[...] = lfn_ref[...]

    lf = lf_ref[...]
    jj = lax.broadcasted_iota(jnp.int32, (LANES, LANES), 0)
    ss = lax.broadcasted_iota(jnp.int32, (LANES, LANES), 1)
    upper = (jj > ss).astype(BF16)
    hi, mid, lo = _split3(lf)
    suffix = _dot(hi, upper) + _dot(mid, upper) + _dot(lo, upper)
    s = _dot_nt(qbd_ref[...], k_ref[...].astype(BF16)) + carry_ref[...] + suffix
    m = m_ref[...]
    m_new = jnp.maximum(m, jnp.max(s, axis=-1, keepdims=True))
    alpha = jnp.exp(m - m_new)
    pr = jnp.exp(s - m_new)
    l_ref[...] = alpha * l_ref[...] + jnp.sum(pr, axis=-1, keepdims=True)
    m_ref[...] = m_new
    acc_ref[...] = alpha * acc_ref[...] + _dot(pr.astype(BF16), v_ref[...].astype(BF16))
    carry_ref[...] = carry_ref[...] + jnp.sum(lf, axis=-1, keepdims=True)

    @pl.when(p == pl.num_programs(1) - 1)
    def _():
        o = acc_ref[...] / l_ref[...]
        r = lax.broadcasted_iota(jnp.int32, (N_HEADS, HEAD_DIM), 0)
        out = jnp.zeros((N_HEADS, HEAD_DIM), F32)
        for c in range(N_HEADS):
            out = out + jnp.where(r == c, o[:, c * HEAD_DIM:(c + 1) * HEAD_DIM], 0.0)
        o_ref[...] = out


def _decode_attn(page_table, q3, kn3, vn_row, lfn_col, cache_k2, cache_v2, cache_lft):
    nb, n_pages = page_table.shape
    _, page, d = cache_k2.shape
    last = n_pages - 1
    page_map = lambda b, p, pt: (pt[b, last - p], 0, 0)
    per_b = lambda b, p, pt: (b, 0, 0)
    return pl.pallas_call(
        _decode_kernel,
        out_shape=jax.ShapeDtypeStruct((nb, N_HEADS, HEAD_DIM), F32),
        grid_spec=pltpu.PrefetchScalarGridSpec(
            num_scalar_prefetch=1,
            grid=(nb, n_pages),
            in_specs=[
                pl.BlockSpec((None, N_HEADS, HEAD_DIM), per_b),
                pl.BlockSpec((None, N_HEADS, HEAD_DIM), per_b),
                pl.BlockSpec((None, 1, d), per_b),
                pl.BlockSpec((None, N_HEADS, 1), per_b),
                pl.BlockSpec((None, page, d), page_map),
                pl.BlockSpec((None, page, d), page_map),
                pl.BlockSpec((None, N_HEADS, page), page_map),
            ],
            out_specs=pl.BlockSpec((None, N_HEADS, HEAD_DIM), per_b),
            scratch_shapes=[
                pltpu.VMEM((N_HEADS, d), BF16),
                pltpu.VMEM((N_HEADS, d), F32),
                pltpu.VMEM((N_HEADS, 1), F32),
                pltpu.VMEM((N_HEADS, 1), F32),
                pltpu.VMEM((N_HEADS, 1), F32),
            ],
        ),
        compiler_params=_params("arbitrary", "arbitrary"),
        name="fox_decode",
    )(page_table, q3, kn3, vn_row, lfn_col, cache_k2, cache_v2, cache_lft)


def _mm_res_kernel(*refs, has_bias):
    if has_bias:
        x_ref, w_ref, b_ref, res_ref, gate_ref, o_ref = refs
    else:
        x_ref, w_ref, res_ref, gate_ref, o_ref = refs
    acc = _dot(x_ref[...], w_ref[...])
    if has_bias:
        acc = acc + b_ref[...]
    o_ref[...] = res_ref[...] + gate_ref[...] * acc


def _mm_res(x, w, b, res, mod, gate_k, rows_per_group, tm, tn):
    m, k = x.shape
    n = w.shape[1]
    tpg = rows_per_group // tm
    ins = [x, w]
    specs = [pl.BlockSpec((tm, k), lambda i, j: (i, 0)), pl.BlockSpec((k, tn), lambda i, j: (0, j))]
    if b is not None:
        ins.append(b.reshape(1, n))
        specs.append(pl.BlockSpec((1, tn), lambda i, j: (0, j)))
    ins += [res, mod]
    r = mod.shape[1]
    nblk = n // tn
    specs += [pl.BlockSpec((tm, tn), lambda i, j: (i, j)),
              pl.BlockSpec((None, r, tn), lambda i, j: (i // tpg, 0, gate_k * nblk + j))]
    return pl.pallas_call(
        functools.partial(_mm_res_kernel, has_bias=b is not None),
        out_shape=jax.ShapeDtypeStruct((m, n), F32),
        grid=(m // tm, n // tn),
        in_specs=specs,
        out_specs=pl.BlockSpec((tm, tn), lambda i, j: (i, j)),
        compiler_params=_params("arbitrary", "arbitrary"),
        name="matmul_gated_residual",
    )(*ins)


def _ff1_kernel(x_ref, g_ref, sh_ref, sc_ref, w_ref, o_ref, h_ref):
    @pl.when(pl.program_id(1) == 0)
    def _():
        h_ref[...] = _modnorm(x_ref[...], g_ref[...], sh_ref[...], sc_ref[...]).astype(BF16)

    z = jnp.maximum(_dot(h_ref[...], w_ref[...]), 0.0)
    o_ref[...] = (z * z).astype(o_ref.dtype)


def _ff1(x, g, mod, rows_per_group, w, tm, tn):
    m, d = x.shape
    n = w.shape[1]
    tpg = rows_per_group // tm
    return pl.pallas_call(
        _ff1_kernel,
        out_shape=jax.ShapeDtypeStruct((m, n), BF16),
        grid=(m // tm, n // tn),
        in_specs=[
            pl.BlockSpec((tm, d), lambda i, j: (i, 0)),
            pl.BlockSpec((1, d), lambda i, j: (0, 0)),
            _mod_spec(mod, 3, d, tpg),
            _mod_spec(mod, 4, d, tpg),
            pl.BlockSpec((d, tn), lambda i, j: (0, j)),
        ],
        out_specs=pl.BlockSpec((tm, tn), lambda i, j: (i, j)),
        scratch_shapes=[pltpu.VMEM((tm, d), BF16)],
        compiler_params=_params("arbitrary", "arbitrary"),
        name="mlp_up",
    )(x, g.reshape(1, d), mod, mod, w)


def _pw1_kernel(x_ref, g_ref, sh_ref, sc_ref, wa_ref, wg_ref, ba_ref, bg_ref, o_ref, h_ref):
    @pl.when(pl.program_id(1) == 0)
    def _():
        h_ref[...] = _modnorm(x_ref[...], g_ref[...], sh_ref[...], sc_ref[...]).astype(BF16)

    hb = h_ref[...]
    a = _dot(hb, wa_ref[...]) + ba_ref[...]
    gt = _dot(hb, wg_ref[...]) + bg_ref[...]
    o_ref[...] = a * jax.nn.sigmoid(gt)


def _pw1(x, g, mod, rows_per_group, w, b, tm, tn):
    m, d = x.shape
    nb = d // tn
    tpg = rows_per_group // tm
    b2 = b.reshape(1, 2 * d)
    return pl.pallas_call(
        _pw1_kernel,
        out_shape=jax.ShapeDtypeStruct((m, d), F32),
        grid=(m // tm, nb),
        in_specs=[
            pl.BlockSpec((tm, d), lambda i, j: (i, 0)),
            pl.BlockSpec((1, d), lambda i, j: (0, 0)),
            _mod_spec(mod, 0, d, tpg),
            _mod_spec(mod, 1, d, tpg),
            pl.BlockSpec((d, tn), lambda i, j: (0, j)),
            pl.BlockSpec((d, tn), lambda i, j: (0, nb + j)),
            pl.BlockSpec((1, tn), lambda i, j: (0, j)),
            pl.BlockSpec((1, tn), lambda i, j: (0, nb + j)),
        ],
        out_specs=pl.BlockSpec((tm, tn), lambda i, j: (i, j)),
        scratch_shapes=[pltpu.VMEM((tm, d), BF16)],
        compiler_params=_params("arbitrary", "arbitrary"),
        name="conv_pw1_glu",
    )(x, g.reshape(1, d), mod, mod, w, w, b2, b2)


def _ln_silu(y, g, b):
    mu = jnp.mean(y, axis=-1, keepdims=True)
    yc = y - mu
    z = yc * lax.rsqrt(jnp.mean(yc * yc, axis=-1, keepdims=True) + EPS) * g + b
    return z * jax.nn.sigmoid(z)


def _conv_kernel(u_ref, w_ref, bdw_ref, g_ref, b_ref, o_ref, win_ref, *, tt):
    @pl.when(pl.program_id(1) == 0)
    def _():
        win_ref[0:CONV_HALO, :] = jnp.zeros((CONV_HALO, win_ref.shape[1]), F32)

    win_ref[CONV_HALO:CONV_HALO + tt, :] = u_ref[...]
    first = CONV_HALO - (CONV_WIDTH - 1)

    def chunk(ci, carry):
        r0 = pl.multiple_of(ci * CONV_ROWS, CONV_ROWS)
        acc = jnp.zeros((CONV_ROWS, win_ref.shape[1]), F32) + bdw_ref[...]
        win = win_ref[pl.ds(r0, CONV_ROWS + CONV_HALO), :]
        for k in range(CONV_WIDTH):
            acc = acc + win[first + k:first + k + CONV_ROWS, :] * w_ref[k:k + 1, :]
        o_ref[pl.ds(r0, CONV_ROWS), :] = _ln_silu(acc, g_ref[...], b_ref[...]).astype(o_ref.dtype)
        return carry

    lax.fori_loop(0, tt // CONV_ROWS, chunk, 0)
    win_ref[0:CONV_HALO, :] = win_ref[tt:tt + CONV_HALO, :]


def _conv_prompt(u, w_dw, b_dw, g_ln, b_ln, tt=512):
    b, t, d = u.shape
    vec = lambda bi, i: (0, 0)
    return pl.pallas_call(
        functools.partial(_conv_kernel, tt=tt),
        out_shape=jax.ShapeDtypeStruct((b, t, d), BF16),
        grid=(b, t // tt),
        in_specs=[
            pl.BlockSpec((None, tt, d), lambda bi, i: (bi, i, 0)),
            pl.BlockSpec((CONV_WIDTH, d), vec),
            pl.BlockSpec((1, d), vec),
            pl.BlockSpec((1, d), vec),
            pl.BlockSpec((1, d), vec),
        ],
        out_specs=pl.BlockSpec((None, tt, d), lambda bi, i: (bi, i, 0)),
        scratch_shapes=[pltpu.VMEM((CONV_HALO + tt, d), F32)],
        compiler_params=_params("arbitrary", "arbitrary"),
        name="conv_dw_prompt",
    )(u, w_dw, b_dw.reshape(1, d), g_ln.reshape(1, d), b_ln.reshape(1, d))


def _conv_sample_kernel(hist_ref, u_ref, w_ref, bdw_ref, g_ref, b_ref, o_ref):
    nb = hist_ref.shape[0]
    nh = CONV_WIDTH - 1
    rows = []
    for bi in range(nb):
        y = jnp.sum(hist_ref[bi] * w_ref[0:nh, :], axis=0, keepdims=True)
        rows.append(y + u_ref[bi:bi + 1, :] * w_ref[nh:nh + 1, :] + bdw_ref[...])
    y = jnp.concatenate(rows, axis=0)
    z = _ln_silu(y, g_ref[...], b_ref[...])
    o_ref[...] = jnp.concatenate([z, jnp.zeros((o_ref.shape[0] - nb, z.shape[1]), F32)], axis=0).astype(o_ref.dtype)


def _conv_sample(hist, u, w_dw, b_dw, g_ln, b_ln):
    d = u.shape[1]
    return pl.pallas_call(
        _conv_sample_kernel,
        out_shape=jax.ShapeDtypeStruct((SAMPLE_ROWS, d), BF16),
        compiler_params=pltpu.CompilerParams(vmem_limit_bytes=VMEM_LIMIT),
        name="conv_dw_sample",
    )(hist, u, w_dw, b_dw.reshape(1, d), g_ln.reshape(1, d), b_ln.reshape(1, d))


def kernel(x_prompt, x_sample, cache_k, cache_v, cache_logf, state_conv, page_table, c_prompt, c_sample,
           w_mod, b_mod, g_mix, g_ffn, w_qkvf, b_f, g_q, g_k, w_o,
           w_pw1, b_pw1, w_dw, b_dw, g_cln, b_cln, w_pw2, b_pw2, w_ff1, w_ff2):
    nbp, t, d = x_prompt.shape
    nbs = x_sample.shape[0]
    depth = w_mod.shape[0]
    mp = nbp * t
    pool, page = cache_k.shape[1], cache_k.shape[2]
    assert x_sample.shape[1] == 1 and nbp + nbs <= SAMPLE_ROWS and nbs <= SAMPLE_ROWS

    w_mod_b, w_qkvf_b, w_o_b = w_mod.astype(BF16), w_qkvf.astype(BF16), w_o.astype(BF16)
    w_pw1_b, w_pw2_b = w_pw1.astype(BF16), w_pw2.astype(BF16)
    w_ff1_b, w_ff2_b = w_ff1.astype(BF16), w_ff2.astype(BF16)

    c16 = jnp.zeros((SAMPLE_ROWS, d), F32).at[:nbp].set(c_prompt).at[nbp:nbp + nbs].set(c_sample)
    mods = _mods(c16, w_mod_b, b_mod)

    xp = x_prompt.reshape(mp, d)
    xs = jnp.zeros((SAMPLE_ROWS, d), F32).at[:nbs].set(x_sample.reshape(nbs, d))

    kp_l, vp_l, lfp_l, cvp_l, ks_l, vs_l, lfs_l, cvs_l = [], [], [], [], [], [], [], []
    tm_p, tm_s = 512, SAMPLE_ROWS
    for i in range(depth):
        j = i // 2
        modp = mods[i, :nbp].reshape(nbp, 1, 6 * d)
        mods_s = jnp.zeros((1, SAMPLE_ROWS, 6 * d), F32).at[0, :nbs].set(mods[i, nbp:nbp + nbs])
        if i % 2 == 0:
            wf_pad = jnp.zeros((d, LANES), BF16).at[:, :N_HEADS].set(w_qkvf_b[j][:, 3 * d:])
            bf_pad = jnp.zeros((1, LANES), F32).at[0, :N_HEADS].set(b_f[j])
            q, k32, kb, v32, vb, lf = _qkvf(xp, g_mix[i], modp, t, w_qkvf_b[j], wf_pad, bf_pad, g_q[j], g_k[j], tm_p)
            lf_p = lf[:, :N_HEADS].reshape(nbp, t, N_HEADS)
            c = _cumsum_rows(jnp.transpose(lf_p, (0, 2, 1)).reshape(nbp * N_HEADS, t))
            o = _flash(q.reshape(nbp, t, d), kb.reshape(nbp, t, d), vb.reshape(nbp, t, d),
                       c.reshape(nbp, N_HEADS, 1, t))
            xp = _mm_res(o.reshape(mp, d), w_o_b[j], None, xp, modp, 2, t, tm_p, 1024)
            kp_l.append(k32.reshape(nbp, t, N_HEADS, HEAD_DIM))
            vp_l.append(v32.reshape(nbp, t, N_HEADS, HEAD_DIM))
            lfp_l.append(lf_p)
            q, k32, kb, v32, vb, lf = _qkvf(xs, g_mix[i], mods_s, SAMPLE_ROWS, w_qkvf_b[j], wf_pad, bf_pad,
                                            g_q[j], g_k[j], tm_s)
            lf_s = lf[:nbs, :N_HEADS]
            o = _decode_attn(page_table,
                             q[:nbs].reshape(nbs, N_HEADS, HEAD_DIM),
                             k32[:nbs].reshape(nbs, N_HEADS, HEAD_DIM),
                             v32[:nbs].reshape(nbs, 1, d),
                             lf_s.reshape(nbs, N_HEADS, 1),
                             cache_k[j].reshape(pool, page, d), cache_v[j].reshape(pool, page, d),
                             jnp.transpose(cache_logf[j], (0, 2, 1)))
            o16 = jnp.zeros((SAMPLE_ROWS, d), BF16).at[:nbs].set(o.reshape(nbs, d).astype(BF16))
            xs = _mm_res(o16, w_o_b[j], None, xs, mods_s, 2, SAMPLE_ROWS, tm_s, 1024)
            ks_l.append(k32[:nbs].reshape(nbs, 1, N_HEADS, HEAD_DIM))
            vs_l.append(v32[:nbs].reshape(nbs, 1, N_HEADS, HEAD_DIM))
            lfs_l.append(lf_s.reshape(nbs, 1, N_HEADS))
        else:
            u = _pw1(xp, g_mix[i], modp, t, w_pw1_b[j], b_pw1[j], tm_p, 512)
            u3 = u.reshape(nbp, t, d)
            act = _conv_prompt(u3, w_dw[j], b_dw[j], g_cln[j], b_cln[j])
            xp = _mm_res(act.reshape(mp, d), w_pw2_b[j], b_pw2[j], xp, modp, 2, t, tm_p, 1024)
            cvp_l.append(u3[:, t - (CONV_WIDTH - 1):])
            u = _pw1(xs, g_mix[i], mods_s, SAMPLE_ROWS, w_pw1_b[j], b_pw1[j], tm_s, 1024)
            act = _conv_sample(state_conv[j], u, w_dw[j], b_dw[j], g_cln[j], b_cln[j])
            xs = _mm_res(act, w_pw2_b[j], b_pw2[j], xs, mods_s, 2, SAMPLE_ROWS, tm_s, 1024)
            cvs_l.append(jnp.concatenate([state_conv[j][:, 1:], u[:nbs, None, :]], axis=1))
        hid = _ff1(xp, g_ffn[i], modp, t, w_ff1_b[i], tm_p, 1024)
        xp = _mm_res(hid, w_ff2_b[i], None, xp, modp, 5, t, tm_p, 512)
        hid = _ff1(xs, g_ffn[i], mods_s, SAMPLE_ROWS, w_ff1_b[i], tm_s, 2048)
        xs = _mm_res(hid, w_ff2_b[i], None, xs, mods_s, 5, SAMPLE_ROWS, tm_s, 512)

    return (xp.reshape(nbp, t, d), xs[:nbs].reshape(nbs, 1, d),
            jnp.stack(kp_l), jnp.stack(vp_l), jnp.stack(lfp_l), jnp.stack(cvp_l),
            jnp.stack(ks_l), jnp.stack(vs_l), jnp.stack(lfs_l), jnp.stack(cvs_l))
```

```python
import functools
import math

import jax
import jax.numpy as jnp
from jax import lax
from jax.experimental import pallas as pl
from jax.experimental.pallas import tpu as pltpu

F32 = jnp.float32
BF16 = jnp.bfloat16

N_HEADS = 16
HEAD_DIM = 128
CONV_WIDTH = 31
EPS = 1e-6
NEG_INF = -1e30
LOG2E = math.log2(math.e)
LANES = 128
SUBLANES = 8
VMEM_LIMIT = 56 * 1024 * 1024
SAMPLE_ROWS = 16
CONV_HALO = 32
CONV_ROWS = 64
PAGES_PER_STEP = 8


def _params(*sem):
    return pltpu.CompilerParams(dimension_semantics=sem, vmem_limit_bytes=VMEM_LIMIT)


def _dot(a, b):
    return jnp.dot(a, b, preferred_element_type=F32)


def _dot_nt(a, b):
    return lax.dot_general(a, b, (((1,), (1,)), ((), ())), preferred_element_type=F32)


def _split3(x):
    hi = x.astype(BF16)
    r1 = x - hi.astype(F32)
    mid = r1.astype(BF16)
    lo = (r1 - mid.astype(F32)).astype(BF16)
    return hi, mid, lo


def _dot3(x, m):
    hi, mid, lo = _split3(x)
    return _dot(hi, m) + _dot(mid, m) + _dot(lo, m)


def _modnorm(x, g, shift, scale):
    ms = jnp.mean(x * x, axis=-1, keepdims=True)
    y = x * lax.rsqrt(ms + EPS) * g
    return y * (1.0 + scale) + shift


def _log_sigmoid(z):
    return jnp.minimum(z, 0.0) - jnp.log1p(jnp.exp(-jnp.abs(z)))


def _mod_spec(mod, k, d, tiles_per_group):
    r = mod.shape[1]
    return pl.BlockSpec((None, r, d), lambda i, *_: (i // tiles_per_group, 0, k))


def _mods_kernel(c_ref, w_ref, b_ref, o_ref):
    c = c_ref[...]
    a = (c * jax.nn.sigmoid(c)).astype(BF16)
    o_ref[...] = _dot(a, w_ref[...].astype(BF16)) + b_ref[...]


def _mods(c16, w_mod, b_mod, tn=1024):
    depth, d, n = w_mod.shape
    return pl.pallas_call(
        _mods_kernel,
        out_shape=jax.ShapeDtypeStruct((depth, SAMPLE_ROWS, n), F32),
        grid=(depth, n // tn),
        in_specs=[
            pl.BlockSpec((SAMPLE_ROWS, d), lambda l, j: (0, 0)),
            pl.BlockSpec((None, d, tn), lambda l, j: (l, 0, j)),
            pl.BlockSpec((None, 1, tn), lambda l, j: (l, 0, j)),
        ],
        out_specs=pl.BlockSpec((None, SAMPLE_ROWS, tn), lambda l, j: (l, 0, j)),
        compiler_params=_params("arbitrary", "arbitrary"),
        name="adaln_mods",
    )(c16, w_mod, b_mod.reshape(depth, 1, n))


def _qkvf_kernel(x_ref, g_ref, sh_ref, sc_ref, wq_ref, wk_ref, wv_ref, wf_ref, bf_ref, gq_ref, gk_ref,
                 q_ref, k32_ref, kb_ref, v32_ref, vb_ref, lf_ref, h_ref, *, tn):
    @pl.when(pl.program_id(1) == 0)
    def _():
        hb = _modnorm(x_ref[...], g_ref[...], sh_ref[...], sc_ref[...]).astype(BF16)
        h_ref[...] = hb
        lf_ref[...] = _log_sigmoid(_dot(hb, wf_ref[...].astype(BF16)) + bf_ref[...])

    hb = h_ref[...]
    zq = _dot(hb, wq_ref[...].astype(BF16))
    zk = _dot(hb, wk_ref[...].astype(BF16))
    zv = _dot(hb, wv_ref[...].astype(BF16))
    q_scale = HEAD_DIM ** -0.5 * LOG2E
    for hh in range(tn // HEAD_DIM):
        sl = slice(hh * HEAD_DIM, (hh + 1) * HEAD_DIM)
        zh = zq[:, sl]
        qn = zh * lax.rsqrt(jnp.mean(zh * zh, axis=-1, keepdims=True) + EPS) * gq_ref[...]
        q_ref[:, sl] = (qn * q_scale).astype(BF16)
        zh = zk[:, sl]
        kn = zh * lax.rsqrt(jnp.mean(zh * zh, axis=-1, keepdims=True) + EPS) * gk_ref[...]
        k32_ref[:, sl] = kn
        kb_ref[:, sl] = kn.astype(BF16)
    v32_ref[...] = zv
    vb_ref[...] = zv.astype(BF16)


def _qkvf(x, g, mod, rows_per_group, w_all, layer, wf_pad, bf_pad, g_q, g_k, tm, tn):
    m, d = x.shape
    nb = d // tn
    tpg = rows_per_group // tm
    row = lambda i, j: (i, j)
    vec = lambda i, j: (0, 0)
    return pl.pallas_call(
        functools.partial(_qkvf_kernel, tn=tn),
        out_shape=(
            jax.ShapeDtypeStruct((m, d), BF16),
            jax.ShapeDtypeStruct((m, d), F32),
            jax.ShapeDtypeStruct((m, d), BF16),
            jax.ShapeDtypeStruct((m, d), F32),
            jax.ShapeDtypeStruct((m, d), BF16),
            jax.ShapeDtypeStruct((m, LANES), F32),
        ),
        grid=(m // tm, nb),
        in_specs=[
            pl.BlockSpec((tm, d), lambda i, j: (i, 0)),
            pl.BlockSpec((1, d), vec),
            _mod_spec(mod, 0, d, tpg),
            _mod_spec(mod, 1, d, tpg),
            pl.BlockSpec((None, d, tn), lambda i, j: (layer, 0, j)),
            pl.BlockSpec((None, d, tn), lambda i, j: (layer, 0, nb + j)),
            pl.BlockSpec((None, d, tn), lambda i, j: (layer, 0, 2 * nb + j)),
            pl.BlockSpec((d, LANES), vec),
            pl.BlockSpec((1, LANES), vec),
            pl.BlockSpec((1, HEAD_DIM), vec),
            pl.BlockSpec((1, HEAD_DIM), vec),
        ],
        out_specs=(
            pl.BlockSpec((tm, tn), row), pl.BlockSpec((tm, tn), row), pl.BlockSpec((tm, tn), row),
            pl.BlockSpec((tm, tn), row), pl.BlockSpec((tm, tn), row),
            pl.BlockSpec((tm, LANES), lambda i, j: (i, 0)),
        ),
        scratch_shapes=[pltpu.VMEM((tm, d), BF16)],
        compiler_params=_params("arbitrary", "arbitrary"),
        name="fox_qkvf",
    )(x, g.reshape(1, d), mod, mod, w_all, w_all, w_all, wf_pad, bf_pad, g_q.reshape(1, -1), g_k.reshape(1, -1))


def _cumsum_kernel(x_ref, o_ref):
    n = x_ref.shape[1]
    j = lax.broadcasted_iota(jnp.int32, (LANES, LANES), 0)
    s = lax.broadcasted_iota(jnp.int32, (LANES, LANES), 1)
    tri = (j <= s).astype(BF16)
    carry = jnp.zeros((x_ref.shape[0], 1), F32)
    for c in range(n // LANES):
        sl = slice(c * LANES, (c + 1) * LANES)
        cs = _dot3(x_ref[:, sl], tri)
        o_ref[:, sl] = cs + carry
        carry = carry + cs[:, LANES - 1:LANES]


def _cumsum_rows(x):
    return pl.pallas_call(
        _cumsum_kernel,
        out_shape=jax.ShapeDtypeStruct(x.shape, F32),
        name="logf_cumsum",
    )(x)


def _flash_kernel(q_ref, k_ref, v_ref, c_ref, o_ref, kaug_ref, caq_ref, vt_ref, acc_ref, *, tq):
    i = pl.program_id(2)
    t = k_ref.shape[0]

    @pl.when(i == 0)
    def _():
        lane = lax.broadcasted_iota(jnp.int32, (LANES, LANES), 1)

        def build(ci, carry):
            r0 = pl.multiple_of(ci * LANES, LANES)
            crow = c_ref[:, pl.ds(r0, LANES)] * LOG2E
            col = jnp.broadcast_to(crow, (LANES, LANES)).T
            hi, mid, lo = [p.astype(F32) for p in _split3(col)]
            pieces_k = jnp.where(lane == 0, hi, jnp.where(lane == 1, mid, jnp.where(lane == 2, lo, 0.0)))
            ones_k = jnp.where((lane >= 3) & (lane < 6), 1.0, 0.0)
            pieces_q = jnp.where(lane == 3, hi, jnp.where(lane == 4, mid, jnp.where(lane == 5, lo, 0.0)))
            ones_q = jnp.where(lane < 3, -1.0, 0.0)
            kaug_ref[pl.ds(r0, LANES), 0:HEAD_DIM] = k_ref[pl.ds(r0, LANES), :]
            kaug_ref[pl.ds(r0, LANES), HEAD_DIM:2 * HEAD_DIM] = (pieces_k + ones_k).astype(BF16)
            caq_ref[pl.ds(r0, LANES), :] = (pieces_q + ones_q).astype(BF16)
            vt_ref[:, pl.ds(r0, LANES)] = v_ref[pl.ds(r0, LANES), :].astype(F32).T.astype(BF16)
            return carry

        lax.fori_loop(0, t // LANES, build, 0)

    q0 = pl.multiple_of(i * tq, tq)
    q_aug = jnp.concatenate([q_ref[...], caq_ref[pl.ds(q0, tq), :]], axis=1)
    acc_ref[...] = jnp.zeros_like(acc_ref)

    def step(k0, m, l, masked):
        s = _dot_nt(kaug_ref[pl.ds(k0, tq), :], q_aug)
        if masked:
            kk = lax.broadcasted_iota(jnp.int32, (tq, tq), 0)
            qq = lax.broadcasted_iota(jnp.int32, (tq, tq), 1)
            s = jnp.where(kk <= qq, s, NEG_INF)
        m_new = jnp.maximum(m, jnp.max(s, axis=0, keepdims=True))
        alpha = jnp.exp2(m - m_new)
        p = jnp.exp2(s - m_new)
        l_new = alpha * l + jnp.sum(p, axis=0, keepdims=True)
        acc_ref[...] = alpha * acc_ref[...] + _dot(vt_ref[:, pl.ds(k0, tq)], p.astype(BF16))
        return m_new, l_new

    def body(j, carry):
        m, l = carry
        return step(pl.multiple_of(j * tq, tq), m, l, False)

    m0 = jnp.full((1, tq), NEG_INF, F32)
    l0 = jnp.zeros((1, tq), F32)
    m, l = lax.fori_loop(0, i, body, (m0, l0))
    m, l = step(q0, m, l, True)
    ot = acc_ref[...] / l
    for c in range(tq // LANES):
        sl = slice(c * LANES, (c + 1) * LANES)
        o_ref[sl, :] = ot[:, sl].T.astype(o_ref.dtype)


def _flash(q, k, v, c, tq=512):
    b, t, d = q.shape
    h = d // HEAD_DIM
    return pl.pallas_call(
        functools.partial(_flash_kernel, tq=tq),
        out_shape=jax.ShapeDtypeStruct((b, t, d), BF16),
        grid=(b, h, t // tq),
        in_specs=[
            pl.BlockSpec((None, tq, HEAD_DIM), lambda bi, hi, i: (bi, i, hi)),
            pl.BlockSpec((None, t, HEAD_DIM), lambda bi, hi, i: (bi, 0, hi)),
            pl.BlockSpec((None, t, HEAD_DIM), lambda bi, hi, i: (bi, 0, hi)),
            pl.BlockSpec((None, None, 1, t), lambda bi, hi, i: (bi, hi, 0, 0)),
        ],
        out_specs=pl.BlockSpec((None, tq, HEAD_DIM), lambda bi, hi, i: (bi, i, hi)),
        scratch_shapes=[
            pltpu.VMEM((t, 2 * HEAD_DIM), BF16),
            pltpu.VMEM((t, HEAD_DIM), BF16),
            pltpu.VMEM((HEAD_DIM, t), BF16),
            pltpu.VMEM((HEAD_DIM, tq), F32),
        ],
        compiler_params=_params("arbitrary", "arbitrary", "arbitrary"),
        name="fox_flash",
    )(q, k, v, c)


def _decode_kernel(pt_ref, q_ref, kn_ref, vn_ref, lfn_ref, *rest, n_grp):
    k_refs, v_refs, lf_refs = rest[:n_grp], rest[n_grp:2 * n_grp], rest[2 * n_grp:3 * n_grp]
    o_ref, acc_ref, m_ref, l_ref, carry_ref = rest[3 * n_grp:]
    p = pl.program_id(1)
    page = k_refs[0].shape[0]
    rows = page * N_HEADS
    grp = rows // LANES

    head_s = lax.broadcasted_iota(jnp.int32, (N_HEADS, LANES), 0)
    lane_h = lax.broadcasted_iota(jnp.int32, (N_HEADS, LANES), 1) % N_HEADS
    own = head_s == lane_h

    def to_pattern(col):
        return jnp.sum(jnp.where(own, col, 0.0), axis=0, keepdims=True)

    def to_column(pat):
        first = lax.broadcasted_iota(jnp.int32, (N_HEADS, LANES), 1) == head_s
        return jnp.sum(jnp.where(first, pat, 0.0), axis=1, keepdims=True)

    @pl.when(p == 0)
    def _():
        kn = kn_ref[...].astype(BF16).astype(F32)
        s_new = jnp.sum(q_ref[...].astype(F32) * kn, axis=-1, keepdims=True)
        m_ref[...] = to_pattern(s_new)
        l_ref[...] = jnp.ones_like(l_ref)
        acc_ref[...] = vn_ref[...].astype(BF16).astype(F32)
        carry_ref[...] = lfn_ref[...] * LOG2E

    a = lax.broadcasted_iota(jnp.int32, (LANES, LANES), 0)
    b = lax.broadcasted_iota(jnp.int32, (LANES, LANES), 1)
    same_head = (a % N_HEADS) == (b % N_HEADS)
    newer_in_row = (same_head & (a // N_HEADS > b // N_HEADS)).astype(BF16)
    all_in_row = same_head.astype(BF16)
    qh = q_ref[...]

    for g in range(n_grp):
        lf = lf_refs[g][...] * LOG2E
        row_tot = _dot3(lf, all_in_row)
        run = carry_ref[...]
        newer = [None] * grp
        for r in reversed(range(grp)):
            newer[r] = run
            run = run + row_tot[r:r + 1, :]
        bias = jnp.concatenate(newer, axis=0) + _dot3(lf, newer_in_row)

        kb = k_refs[g][...].reshape(rows, HEAD_DIM).astype(BF16)
        st = _dot_nt(qh, kb)
        srows = []
        for r in range(grp):
            blk = st[:, r * LANES:(r + 1) * LANES]
            srows.append(jnp.sum(jnp.where(own, blk, 0.0), axis=0, keepdims=True))
        s = jnp.concatenate(srows, axis=0) + bias

        mx = jnp.max(s, axis=0, keepdims=True)
        for sh in (N_HEADS, 2 * N_HEADS, 4 * N_HEADS):
            mx = jnp.maximum(mx, pltpu.roll(mx, sh, axis=1))
        m_new = jnp.maximum(m_ref[...], mx)
        alpha = jnp.exp2(m_ref[...] - m_new)
        pr = jnp.exp2(s - m_new)
        ps = jnp.sum(pr, axis=0, keepdims=True)
        for sh in (N_HEADS, 2 * N_HEADS, 4 * N_HEADS):
            ps = ps + pltpu.roll(ps, sh, axis=1)
        l_ref[...] = alpha * l_ref[...] + ps
        m_ref[...] = m_new

        pcols = [jnp.where(own, jnp.broadcast_to(pr[r:r + 1, :], (N_HEADS, LANES)), 0.0) for r in range(grp)]
        pmat = jnp.concatenate(pcols, axis=1).astype(BF16)
        vb = v_refs[g][...].reshape(rows, HEAD_DIM).astype(BF16)
        acc_ref[...] = to_column(alpha) * acc_ref[...] + _dot(pmat, vb)
        carry_ref[...] = run

    @pl.when(p == pl.num_programs(1) - 1)
    def _():
        o_ref[...] = acc_ref[...] / to_column(l_ref[...])


def _decode_attn(page_table, layer, q3, kn3, vn3, lfn_pat, cache_k, cache_v, cache_lf, n_grp=PAGES_PER_STEP):
    nb, n_pages = page_table.shape
    page = cache_k.shape[2]
    grp = page * N_HEADS // LANES
    last = n_pages - 1
    per_b = lambda b, p, pt: (b, 0, 0)

    def kv_spec(g):
        return pl.BlockSpec((None, None, page, N_HEADS, HEAD_DIM),
                            lambda b, p, pt: (layer, pt[b, last - (p * n_grp + g)], 0, 0, 0))

    def lf_spec(g):
        return pl.BlockSpec((None, None, grp, LANES),
                            lambda b, p, pt: (layer, pt[b, last - (p * n_grp + g)], 0, 0))

    return pl.pallas_call(
        functools.partial(_decode_kernel, n_grp=n_grp),
        out_shape=jax.ShapeDtypeStruct((nb, N_HEADS, HEAD_DIM), F32),
        grid_spec=pltpu.PrefetchScalarGridSpec(
            num_scalar_prefetch=1,
            grid=(nb, n_pages // n_grp),
            in_specs=[
                pl.BlockSpec((None, N_HEADS, HEAD_DIM), per_b),
                pl.BlockSpec((None, N_HEADS, HEAD_DIM), per_b),
                pl.BlockSpec((None, N_HEADS, HEAD_DIM), per_b),
                pl.BlockSpec((None, 1, LANES), per_b),
            ] + [kv_spec(g) for g in range(n_grp)] + [kv_spec(g) for g in range(n_grp)]
              + [lf_spec(g) for g in range(n_grp)],
            out_specs=pl.BlockSpec((None, N_HEADS, HEAD_DIM), per_b),
            scratch_shapes=[
                pltpu.VMEM((N_HEADS, HEAD_DIM), F32),
                pltpu.VMEM((1, LANES), F32),
                pltpu.VMEM((1, LANES), F32),
                pltpu.VMEM((1, LANES), F32),
            ],
        ),
        compiler_params=_params("arbitrary", "arbitrary"),
        name="fox_decode",
    )(page_table, q3, kn3, vn3, lfn_pat, *([cache_k] * n_grp), *([cache_v] * n_grp), *([cache_lf] * n_grp))


def _mm_res_kernel(*refs, has_bias):
    if has_bias:
        x_ref, w_ref, b_ref, res_ref, gate_ref, o_ref, acc_ref = refs
    else:
        x_ref, w_ref, res_ref, gate_ref, o_ref, acc_ref = refs
    kk = pl.program_id(2)

    @pl.when(kk == 0)
    def _():
        acc_ref[...] = jnp.zeros_like(acc_ref)

    acc_ref[...] += _dot(x_ref[...], w_ref[...].astype(BF16))

    @pl.when(kk == pl.num_programs(2) - 1)
    def _():
        acc = acc_ref[...]
        if has_bias:
            acc = acc + b_ref[...]
        o_ref[...] = res_ref[...] + gate_ref[...] * acc


def _mm_res(x, w_all, layer, b, res, mod, gate_k, rows_per_group, tm, tn, tk):
    m, k = x.shape
    n = w_all.shape[2]
    tpg = rows_per_group // tm
    ins = [x, w_all]
    specs = [pl.BlockSpec((tm, tk), lambda i, j, kk: (i, kk)),
             pl.BlockSpec((None, tk, tn), lambda i, j, kk: (layer, kk, j))]
    if b is not None:
        ins.append(b.reshape(1, n))
        specs.append(pl.BlockSpec((1, tn), lambda i, j, kk: (0, j)))
    ins += [res, mod]
    r = mod.shape[1]
    nblk = n // tn
    specs += [pl.BlockSpec((tm, tn), lambda i, j, kk: (i, j)),
              pl.BlockSpec((None, r, tn), lambda i, j, kk: (i // tpg, 0, gate_k * nblk + j))]
    return pl.pallas_call(
        functools.partial(_mm_res_kernel, has_bias=b is not None),
        out_shape=jax.ShapeDtypeStruct((m, n), F32),
        grid=(m // tm, n // tn, k // tk),
        in_specs=specs,
        out_specs=pl.BlockSpec((tm, tn), lambda i, j, kk: (i, j)),
        scratch_shapes=[pltpu.VMEM((tm, tn), F32)],
        compiler_params=_params("arbitrary", "arbitrary", "arbitrary"),
        name="matmul_gated_residual",
    )(*ins)


def _ff1_kernel(x_ref, g_ref, sh_ref, sc_ref, w_ref, o_ref, h_ref):
    @pl.when(pl.program_id(1) == 0)
    def _():
        h_ref[...] = _modnorm(x_ref[...], g_ref[...], sh_ref[...], sc_ref[...]).astype(BF16)

    z = jnp.maximum(_dot(h_ref[...], w_ref[...].astype(BF16)), 0.0)
    o_ref[...] = (z * z).astype(o_ref.dtype)


def _ff1(x, g, mod, rows_per_group, w_all, layer, tm, tn):
    m, d = x.shape
    n = w_all.shape[2]
    tpg = rows_per_group // tm
    return pl.pallas_call(
        _ff1_kernel,
        out_shape=jax.ShapeDtypeStruct((m, n), BF16),
        grid=(m // tm, n // tn),
        in_specs=[
            pl.BlockSpec((tm, d), lambda i, j: (i, 0)),
            pl.BlockSpec((1, d), lambda i, j: (0, 0)),
            _mod_spec(mod, 3, d, tpg),
            _mod_spec(mod, 4, d, tpg),
            pl.BlockSpec((None, d, tn), lambda i, j: (layer, 0, j)),
        ],
        out_specs=pl.BlockSpec((tm, tn), lambda i, j: (i, j)),
        scratch_shapes=[pltpu.VMEM((tm, d), BF16)],
        compiler_params=_params("arbitrary", "arbitrary"),
        name="mlp_up",
    )(x, g.reshape(1, d), mod, mod, w_all)


def _pw1_kernel(x_ref, g_ref, sh_ref, sc_ref, wa_ref, wg_ref, ba_ref, bg_ref, o_ref, h_ref):
    @pl.when(pl.program_id(1) == 0)
    def _():
        h_ref[...] = _modnorm(x_ref[...], g_ref[...], sh_ref[...], sc_ref[...]).astype(BF16)

    hb = h_ref[...]
    a = _dot(hb, wa_ref[...].astype(BF16)) + ba_ref[...]
    gt = _dot(hb, wg_ref[...].astype(BF16)) + bg_ref[...]
    o_ref[...] = a * jax.nn.sigmoid(gt)


def _pw1(x, g, mod, rows_per_group, w_all, b_all, layer, tm, tn):
    m, d = x.shape
    nb = d // tn
    tpg = rows_per_group // tm
    b3 = b_all.reshape(b_all.shape[0], 1, 2 * d)
    return pl.pallas_call(
        _pw1_kernel,
        out_shape=jax.ShapeDtypeStruct((m, d), F32),
        grid=(m // tm, nb),
        in_specs=[
            pl.BlockSpec((tm, d), lambda i, j: (i, 0)),
            pl.BlockSpec((1, d), lambda i, j: (0, 0)),
            _mod_spec(mod, 0, d, tpg),
            _mod_spec(mod, 1, d, tpg),
            pl.BlockSpec((None, d, tn), lambda i, j: (layer, 0, j)),
            pl.BlockSpec((None, d, tn), lambda i, j: (layer, 0, nb + j)),
            pl.BlockSpec((None, 1, tn), lambda i, j: (layer, 0, j)),
            pl.BlockSpec((None, 1, tn), lambda i, j: (layer, 0, nb + j)),
        ],
        out_specs=pl.BlockSpec((tm, tn), lambda i, j: (i, j)),
        scratch_shapes=[pltpu.VMEM((tm, d), BF16)],
        compiler_params=_params("arbitrary", "arbitrary"),
        name="conv_pw1_glu",
    )(x, g.reshape(1, d), mod, mod, w_all, w_all, b3, b3)


def _ln_silu(y, g, b):
    mu = jnp.mean(y, axis=-1, keepdims=True)
    yc = y - mu
    z = yc * lax.rsqrt(jnp.mean(yc * yc, axis=-1, keepdims=True) + EPS) * g + b
    return z * jax.nn.sigmoid(z)


def _conv_kernel(u_ref, w_ref, bdw_ref, g_ref, b_ref, o_ref, win_ref, y_ref, *, tt):
    d = win_ref.shape[1]

    @pl.when(pl.program_id(1) == 0)
    def _():
        win_ref[0:CONV_HALO, :] = jnp.zeros((CONV_HALO, d), F32)

    win_ref[CONV_HALO:CONV_HALO + tt, :] = u_ref[...]
    win_ref[CONV_HALO + tt:, :] = jnp.zeros((SUBLANES, d), F32)
    first = CONV_HALO - (CONV_WIDTH - 1)
    span = CONV_ROWS + SUBLANES

    def chunk(idx, carry):
        r0 = pl.multiple_of((idx // (d // LANES)) * CONV_ROWS, CONV_ROWS)
        c0 = pl.multiple_of((idx % (d // LANES)) * LANES, LANES)
        acc = jnp.zeros((CONV_ROWS, LANES), F32)
        for r in range(SUBLANES):
            z = None
            for k in range(CONV_WIDTH):
                off = first + k
                if off % SUBLANES != r:
                    continue
                term = win_ref[pl.ds(r0 + (off - r), span), pl.ds(c0, LANES)] * w_ref[k:k + 1, pl.ds(c0, LANES)]
                z = term if z is None else z + term
            if r:
                z = pltpu.roll(z, span - r, axis=0)
            acc = acc + z[:CONV_ROWS]
        y_ref[pl.ds(r0, CONV_ROWS), pl.ds(c0, LANES)] = acc
        return carry

    lax.fori_loop(0, (tt // CONV_ROWS) * (d // LANES), chunk, 0)

    nr = 2 * SUBLANES

    def norm(ci, carry):
        r0 = pl.multiple_of(ci * nr, nr)
        y = y_ref[pl.ds(r0, nr), :] + bdw_ref[...]
        o_ref[pl.ds(r0, nr), :] = _ln_silu(y, g_ref[...], b_ref[...]).astype(o_ref.dtype)
        return carry

    lax.fori_loop(0, tt // nr, norm, 0)
    win_ref[0:CONV_HALO, :] = win_ref[tt:tt + CONV_HALO, :]


def _conv_prompt(u, w_dw, b_dw, g_ln, b_ln, tt=512):
    b, t, d = u.shape
    vec = lambda bi, i: (0, 0)
    return pl.pallas_call(
        functools.partial(_conv_kernel, tt=tt),
        out_shape=jax.ShapeDtypeStruct((b, t, d), BF16),
        grid=(b, t // tt),
        in_specs=[
            pl.BlockSpec((None, tt, d), lambda bi, i: (bi, i, 0)),
            pl.BlockSpec((CONV_WIDTH, d), vec),
            pl.BlockSpec((1, d), vec),
            pl.BlockSpec((1, d), vec),
            pl.BlockSpec((1, d), vec),
        ],
        out_specs=pl.BlockSpec((None, tt, d), lambda bi, i: (bi, i, 0)),
        scratch_shapes=[pltpu.VMEM((CONV_HALO + tt + SUBLANES, d), F32), pltpu.VMEM((tt, d), F32)],
        compiler_params=_params("arbitrary", "arbitrary"),
        name="conv_dw_prompt",
    )(u, w_dw, b_dw.reshape(1, d), g_ln.reshape(1, d), b_ln.reshape(1, d))


def _conv_sample_kernel(hist_ref, u_ref, w_ref, bdw_ref, g_ref, b_ref, o_ref):
    nb = hist_ref.shape[0]
    nh = CONV_WIDTH - 1
    rows = []
    for bi in range(nb):
        y = jnp.sum(hist_ref[bi] * w_ref[0:nh, :], axis=0, keepdims=True)
        rows.append(y + u_ref[bi:bi + 1, :] * w_ref[nh:nh + 1, :] + bdw_ref[...])
    y = jnp.concatenate(rows, axis=0)
    z = _ln_silu(y, g_ref[...], b_ref[...])
    o_ref[...] = jnp.concatenate([z, jnp.zeros((o_ref.shape[0] - nb, z.shape[1]), F32)], axis=0).astype(o_ref.dtype)


def _conv_sample(hist, u, w_dw, b_dw, g_ln, b_ln):
    d = u.shape[1]
    return pl.pallas_call(
        _conv_sample_kernel,
        out_shape=jax.ShapeDtypeStruct((SAMPLE_ROWS, d), BF16),
        compiler_params=pltpu.CompilerParams(vmem_limit_bytes=VMEM_LIMIT),
        name="conv_dw_sample",
    )(hist, u, w_dw, b_dw.reshape(1, d), g_ln.reshape(1, d), b_ln.reshape(1, d))


def kernel(x_prompt, x_sample, cache_k, cache_v, cache_logf, state_conv, page_table, c_prompt, c_sample,
           w_mod, b_mod, g_mix, g_ffn, w_qkvf, b_f, g_q, g_k, w_o,
           w_pw1, b_pw1, w_dw, b_dw, g_cln, b_cln, w_pw2, b_pw2, w_ff1, w_ff2):
    nbp, t, d = x_prompt.shape
    nbs = x_sample.shape[0]
    depth = w_mod.shape[0]
    n_fox = cache_k.shape[0]
    mp = nbp * t
    pool, page = cache_k.shape[1], cache_k.shape[2]
    assert x_sample.shape[1] == 1 and nbp + nbs <= SAMPLE_ROWS
    assert LANES % N_HEADS == 0 and (page * N_HEADS) % LANES == 0

    c16 = jnp.zeros((SAMPLE_ROWS, d), F32).at[:nbp].set(c_prompt).at[nbp:nbp + nbs].set(c_sample)
    mods = _mods(c16, w_mod, b_mod)

    xp = x_prompt.reshape(mp, d)
    xs = jnp.zeros((SAMPLE_ROWS, d), F32).at[:nbs].set(x_sample.reshape(nbs, d))

    wf_pad = jnp.zeros((n_fox, d, LANES), F32).at[:, :, :N_HEADS].set(w_qkvf[:, :, 3 * d:])
    bf_pad = jnp.zeros((n_fox, 1, LANES), F32).at[:, 0, :N_HEADS].set(b_f)
    cache_lf = cache_logf.reshape(n_fox, pool, page * N_HEADS // LANES, LANES)

    kp_l, vp_l, lfp_l, cvp_l, ks_l, vs_l, lfs_l, cvs_l = [], [], [], [], [], [], [], []
    tm_p, tm_s = 1024, SAMPLE_ROWS
    for i in range(depth):
        j = i // 2
        modp = mods[i, :nbp].reshape(nbp, 1, 6 * d)
        mods_s = jnp.zeros((1, SAMPLE_ROWS, 6 * d), F32).at[0, :nbs].set(mods[i, nbp:nbp + nbs])
        if i % 2 == 0:
            q, k32, kb, v32, vb, lf = _qkvf(xp, g_mix[i], modp, t, w_qkvf, j, wf_pad[j], bf_pad[j], g_q[j], g_k[j],
                                            tm_p, 256)
            lf_p = lf[:, :N_HEADS].reshape(nbp, t, N_HEADS)
            c = _cumsum_rows(jnp.transpose(lf_p, (0, 2, 1)).reshape(nbp * N_HEADS, t))
            o = _flash(q.reshape(nbp, t, d), kb.reshape(nbp, t, d), vb.reshape(nbp, t, d),
                       c.reshape(nbp, N_HEADS, 1, t))
            xp = _mm_res(o.reshape(mp, d), w_o, j, None, xp, modp, 2, t, tm_p, 1024, 1024)
            kp_l.append(k32.reshape(nbp, t, N_HEADS, HEAD_DIM))
            vp_l.append(v32.reshape(nbp, t, N_HEADS, HEAD_DIM))
            lfp_l.append(lf_p)
            q, k32, kb, v32, vb, lf = _qkvf(xs, g_mix[i], mods_s, SAMPLE_ROWS, w_qkvf, j, wf_pad[j], bf_pad[j],
                                            g_q[j], g_k[j], tm_s, 512)
            lf_s = lf[:nbs, :N_HEADS]
            o = _decode_attn(page_table, j,
                             q[:nbs].reshape(nbs, N_HEADS, HEAD_DIM),
                             k32[:nbs].reshape(nbs, N_HEADS, HEAD_DIM),
                             v32[:nbs].reshape(nbs, N_HEADS, HEAD_DIM),
                             jnp.tile(lf_s, (1, LANES // N_HEADS)).reshape(nbs, 1, LANES),
                             cache_k, cache_v, cache_lf)
            o16 = jnp.zeros((SAMPLE_ROWS, d), BF16).at[:nbs].set(o.reshape(nbs, d).astype(BF16))
            xs = _mm_res(o16, w_o, j, None, xs, mods_s, 2, SAMPLE_ROWS, tm_s, 1024, d)
            ks_l.append(k32[:nbs].reshape(nbs, 1, N_HEADS, HEAD_DIM))
            vs_l.append(v32[:nbs].reshape(nbs, 1, N_HEADS, HEAD_DIM))
            lfs_l.append(lf_s.reshape(nbs, 1, N_HEADS))
        else:
            u = _pw1(xp, g_mix[i], modp, t, w_pw1, b_pw1, j, tm_p, 512)
            u3 = u.reshape(nbp, t, d)
            act = _conv_prompt(u3, w_dw[j], b_dw[j], g_cln[j], b_cln[j])
            xp = _mm_res(act.reshape(mp, d), w_pw2, j, b_pw2[j], xp, modp, 2, t, tm_p, 1024, 1024)
            cvp_l.append(u3[:, t - (CONV_WIDTH - 1):])
            u = _pw1(xs, g_mix[i], mods_s, SAMPLE_ROWS, w_pw1, b_pw1, j, tm_s, 1024)
            act = _conv_sample(state_conv[j], u, w_dw[j], b_dw[j], g_cln[j], b_cln[j])
            xs = _mm_res(act, w_pw2, j, b_pw2[j], xs, mods_s, 2, SAMPLE_ROWS, tm_s, 1024, d)
            cvs_l.append(jnp.concatenate([state_conv[j][:, 1:], u[:nbs, None, :]], axis=1))
        hid = _ff1(xp, g_ffn[i], modp, t, w_ff1, i, tm_p, 512)
        xp = _mm_res(hid, w_ff2, i, None, xp, modp, 5, t, tm_p, 1024, 1024)
        hid = _ff1(xs, g_ffn[i], mods_s, SAMPLE_ROWS, w_ff1, i, tm_s, 1024)
        xs = _mm_res(hid, w_ff2, i, None, xs, mods_s, 5, SAMPLE_ROWS, tm_s, 1024, 2048)

    return (xp.reshape(nbp, t, d), xs[:nbs].reshape(nbs, 1, d),
            jnp.stack(kp_l), jnp.stack(vp_l), jnp.stack(lfp_l), jnp.stack(cvp_l),
            jnp.stack(ks_l), jnp.stack(vs_l), jnp.stack(lfs_l), jnp.stack(cvs_l))
```

```python
import functools
import math

import jax
import jax.numpy as jnp
from jax import lax
from jax.experimental import pallas as pl
from jax.experimental.pallas import tpu as pltpu

F32 = jnp.float32
BF16 = jnp.bfloat16

N_HEADS = 16
HEAD_DIM = 128
CONV_WIDTH = 31
EPS = 1e-6
NEG_INF = -1e30
LOG2E = math.log2(math.e)
LANES = 128
SUBLANES = 8
VMEM_LIMIT = 56 * 1024 * 1024
SAMPLE_ROWS = 16
CONV_HALO = 32
CONV_ROWS = 64
PAGES_PER_STEP = 8


def _params(*sem):
    return pltpu.CompilerParams(dimension_semantics=sem, vmem_limit_bytes=VMEM_LIMIT)


def _dot(a, b):
    return jnp.dot(a, b, preferred_element_type=F32)


def _dot_nt(a, b):
    return lax.dot_general(a, b, (((1,), (1,)), ((), ())), preferred_element_type=F32)


def _split3(x):
    hi = x.astype(BF16)
    r1 = x - hi.astype(F32)
    mid = r1.astype(BF16)
    lo = (r1 - mid.astype(F32)).astype(BF16)
    return hi, mid, lo


def _dot3(x, m):
    hi, mid, lo = _split3(x)
    return _dot(hi, m) + _dot(mid, m) + _dot(lo, m)


def _modnorm(x, g, shift, scale):
    ms = jnp.mean(x * x, axis=-1, keepdims=True)
    y = x * lax.rsqrt(ms + EPS) * g
    return y * (1.0 + scale) + shift


def _log_sigmoid(z):
    return jnp.minimum(z, 0.0) - jnp.log1p(jnp.exp(-jnp.abs(z)))


def _mod_spec(mod, k, d, tiles_per_group):
    r = mod.shape[1]
    return pl.BlockSpec((None, r, d), lambda i, *_: (i // tiles_per_group, 0, k))


def _mods_kernel(c_ref, w_ref, b_ref, o_ref):
    c = c_ref[...]
    a = (c * jax.nn.sigmoid(c)).astype(BF16)
    o_ref[...] = _dot(a, w_ref[...].astype(BF16)) + b_ref[...]


def _mods(c16, w_mod, b_mod, tn=1024):
    depth, d, n = w_mod.shape
    return pl.pallas_call(
        _mods_kernel,
        out_shape=jax.ShapeDtypeStruct((depth, SAMPLE_ROWS, n), F32),
        grid=(depth, n // tn),
        in_specs=[
            pl.BlockSpec((SAMPLE_ROWS, d), lambda l, j: (0, 0)),
            pl.BlockSpec((None, d, tn), lambda l, j: (l, 0, j)),
            pl.BlockSpec((None, 1, tn), lambda l, j: (l, 0, j)),
        ],
        out_specs=pl.BlockSpec((None, SAMPLE_ROWS, tn), lambda l, j: (l, 0, j)),
        compiler_params=_params("arbitrary", "arbitrary"),
        name="adaln_mods",
    )(c16, w_mod, b_mod.reshape(depth, 1, n))


def _qkvf_kernel(x_ref, g_ref, sh_ref, sc_ref, wq_ref, wk_ref, wv_ref, wf_ref, bf_ref, gq_ref, gk_ref,
                 q_ref, k32_ref, kb_ref, v32_ref, vb_ref, lf_ref, h_ref, *, tn):
    @pl.when(pl.program_id(1) == 0)
    def _():
        hb = _modnorm(x_ref[...], g_ref[...], sh_ref[...], sc_ref[...]).astype(BF16)
        h_ref[...] = hb
        lf_ref[...] = _log_sigmoid(_dot(hb, wf_ref[...].astype(BF16)) + bf_ref[...])

    hb = h_ref[...]
    zq = _dot(hb, wq_ref[...].astype(BF16))
    zk = _dot(hb, wk_ref[...].astype(BF16))
    zv = _dot(hb, wv_ref[...].astype(BF16))
    q_scale = HEAD_DIM ** -0.5 * LOG2E
    for hh in range(tn // HEAD_DIM):
        sl = slice(hh * HEAD_DIM, (hh + 1) * HEAD_DIM)
        zh = zq[:, sl]
        qn = zh * lax.rsqrt(jnp.mean(zh * zh, axis=-1, keepdims=True) + EPS) * gq_ref[...]
        q_ref[:, sl] = (qn * q_scale).astype(BF16)
        zh = zk[:, sl]
        kn = zh * lax.rsqrt(jnp.mean(zh * zh, axis=-1, keepdims=True) + EPS) * gk_ref[...]
        k32_ref[:, sl] = kn
        kb_ref[:, sl] = kn.astype(BF16)
    v32_ref[...] = zv
    vb_ref[...] = zv.astype(BF16)


def _qkvf(x, g, mod, rows_per_group, w_all, layer, wf_pad, bf_pad, g_q, g_k, tm, tn):
    m, d = x.shape
    nb = d // tn
    tpg = rows_per_group // tm
    row = lambda i, j: (i, j)
    vec = lambda i, j: (0, 0)
    return pl.pallas_call(
        functools.partial(_qkvf_kernel, tn=tn),
        out_shape=(
            jax.ShapeDtypeStruct((m, d), BF16),
            jax.ShapeDtypeStruct((m, d), F32),
            jax.ShapeDtypeStruct((m, d), BF16),
            jax.ShapeDtypeStruct((m, d), F32),
            jax.ShapeDtypeStruct((m, d), BF16),
            jax.ShapeDtypeStruct((m, LANES), F32),
        ),
        grid=(m // tm, nb),
        in_specs=[
            pl.BlockSpec((tm, d), lambda i, j: (i, 0)),
            pl.BlockSpec((1, d), vec),
            _mod_spec(mod, 0, d, tpg),
            _mod_spec(mod, 1, d, tpg),
            pl.BlockSpec((None, d, tn), lambda i, j: (layer, 0, j)),
            pl.BlockSpec((None, d, tn), lambda i, j: (layer, 0, nb + j)),
            pl.BlockSpec((None, d, tn), lambda i, j: (layer, 0, 2 * nb + j)),
            pl.BlockSpec((d, LANES), vec),
            pl.BlockSpec((1, LANES), vec),
            pl.BlockSpec((1, HEAD_DIM), vec),
            pl.BlockSpec((1, HEAD_DIM), vec),
        ],
        out_specs=(
            pl.BlockSpec((tm, tn), row), pl.BlockSpec((tm, tn), row), pl.BlockSpec((tm, tn), row),
            pl.BlockSpec((tm, tn), row), pl.BlockSpec((tm, tn), row),
            pl.BlockSpec((tm, LANES), lambda i, j: (i, 0)),
        ),
        scratch_shapes=[pltpu.VMEM((tm, d), BF16)],
        compiler_params=_params("arbitrary", "arbitrary"),
        name="fox_qkvf",
    )(x, g.reshape(1, d), mod, mod, w_all, w_all, w_all, wf_pad, bf_pad, g_q.reshape(1, -1), g_k.reshape(1, -1))


def _cumsum_kernel(x_ref, o_ref):
    n = x_ref.shape[1]
    j = lax.broadcasted_iota(jnp.int32, (LANES, LANES), 0)
    s = lax.broadcasted_iota(jnp.int32, (LANES, LANES), 1)
    tri = (j <= s).astype(BF16)
    carry = jnp.zeros((x_ref.shape[0], 1), F32)
    for c in range(n // LANES):
        sl = slice(c * LANES, (c + 1) * LANES)
        cs = _dot3(x_ref[:, sl], tri)
        o_ref[:, sl] = cs + carry
        carry = carry + cs[:, LANES - 1:LANES]


def _cumsum_rows(x):
    return pl.pallas_call(
        _cumsum_kernel,
        out_shape=jax.ShapeDtypeStruct(x.shape, F32),
        name="logf_cumsum",
    )(x)


def _flash_kernel(q_ref, k_ref, v_ref, c_ref, o_ref, kaug_ref, caq_ref, vt_ref, acc_ref,
                  sa_ref, sb_ref, pa_ref, pb_ref, *, tq):
    i = pl.program_id(2)
    t = k_ref.shape[0]

    @pl.when(i == 0)
    def _():
        lane = lax.broadcasted_iota(jnp.int32, (LANES, LANES), 1)

        def build(ci, carry):
            r0 = pl.multiple_of(ci * LANES, LANES)
            crow = c_ref[:, pl.ds(r0, LANES)] * LOG2E
            col = jnp.broadcast_to(crow, (LANES, LANES)).T
            hi, mid, lo = [p.astype(F32) for p in _split3(col)]
            pieces_k = jnp.where(lane == 0, hi, jnp.where(lane == 1, mid, jnp.where(lane == 2, lo, 0.0)))
            ones_k = jnp.where((lane >= 3) & (lane < 6), 1.0, 0.0)
            pieces_q = jnp.where(lane == 3, hi, jnp.where(lane == 4, mid, jnp.where(lane == 5, lo, 0.0)))
            ones_q = jnp.where(lane < 3, -1.0, 0.0)
            kaug_ref[pl.ds(r0, LANES), 0:HEAD_DIM] = k_ref[pl.ds(r0, LANES), :]
            kaug_ref[pl.ds(r0, LANES), HEAD_DIM:2 * HEAD_DIM] = (pieces_k + ones_k).astype(BF16)
            caq_ref[pl.ds(r0, LANES), :] = (pieces_q + ones_q).astype(BF16)
            vt_ref[:, pl.ds(r0, LANES)] = v_ref[pl.ds(r0, LANES), :].astype(F32).T.astype(BF16)
            return carry

        lax.fori_loop(0, t // LANES, build, 0)

    q0 = pl.multiple_of(i * tq, tq)
    q_aug = jnp.concatenate([q_ref[...], caq_ref[pl.ds(q0, tq), :]], axis=1)
    tk = tq // 2
    acc_ref[...] = jnp.zeros_like(acc_ref)
    pb_ref[...] = jnp.zeros_like(pb_ref)

    def qk(k0):
        return _dot_nt(kaug_ref[pl.ds(k0, tk), :], q_aug)

    def soft(s, m, l, mask_from):
        if mask_from is not None:
            kk = lax.broadcasted_iota(jnp.int32, (tk, tq), 0) + mask_from
            qq = lax.broadcasted_iota(jnp.int32, (tk, tq), 1)
            s = jnp.where(kk <= qq, s, NEG_INF)
        m_new = jnp.maximum(m, jnp.max(s, axis=0, keepdims=True))
        alpha = jnp.exp2(m - m_new)
        p = jnp.exp2(s - m_new)
        return m_new, alpha * l + jnp.sum(p, axis=0, keepdims=True), alpha, p.astype(BF16)

    def pv(alpha, k0, p):
        acc_ref[...] = alpha * acc_ref[...] + _dot(vt_ref[:, pl.ds(k0, tk)], p)

    def pair(jj, carry):
        m, l, alpha = carry
        ka = pl.multiple_of(jj * tq, tq)
        sb_ref[...] = qk(ka + tk)
        pv(alpha, pl.multiple_of(jnp.maximum(ka - tk, 0), tk), pb_ref[...])
        m, l, alpha, p = soft(sa_ref[...], m, l, None)
        pa_ref[...] = p
        sa_ref[...] = qk(ka + tq)
        pv(alpha, ka, pa_ref[...])
        m, l, alpha, p = soft(sb_ref[...], m, l, None)
        pb_ref[...] = p
        return m, l, alpha

    sa_ref[...] = qk(0)
    m0 = jnp.full((1, tq), NEG_INF, F32)
    l0 = jnp.zeros((1, tq), F32)
    m, l, alpha = lax.fori_loop(0, i, pair, (m0, l0, jnp.ones((1, tq), F32)))
    sb_ref[...] = qk(q0 + tk)
    pv(alpha, pl.multiple_of(jnp.maximum(q0 - tk, 0), tk), pb_ref[...])
    m, l, alpha, p = soft(sa_ref[...], m, l, 0)
    pv(alpha, q0, p)
    m, l, alpha, p = soft(sb_ref[...], m, l, tk)
    pv(alpha, q0 + tk, p)
    ot = acc_ref[...] / l
    for c in range(tq // LANES):
        sl = slice(c * LANES, (c + 1) * LANES)
        o_ref[sl, :] = ot[:, sl].T.astype(o_ref.dtype)


def _flash(q, k, v, c, tq=512):
    b, t, d = q.shape
    h = d // HEAD_DIM
    return pl.pallas_call(
        functools.partial(_flash_kernel, tq=tq),
        out_shape=jax.ShapeDtypeStruct((b, t, d), BF16),
        grid=(b, h, t // tq),
        in_specs=[
            pl.BlockSpec((None, tq, HEAD_DIM), lambda bi, hi, i: (bi, i, hi)),
            pl.BlockSpec((None, t, HEAD_DIM), lambda bi, hi, i: (bi, 0, hi)),
            pl.BlockSpec((None, t, HEAD_DIM), lambda bi, hi, i: (bi, 0, hi)),
            pl.BlockSpec((None, None, 1, t), lambda bi, hi, i: (bi, hi, 0, 0)),
        ],
        out_specs=pl.BlockSpec((None, tq, HEAD_DIM), lambda bi, hi, i: (bi, i, hi)),
        scratch_shapes=[
            pltpu.VMEM((t, 2 * HEAD_DIM), BF16),
            pltpu.VMEM((t, HEAD_DIM), BF16),
            pltpu.VMEM((HEAD_DIM, t), BF16),
            pltpu.VMEM((HEAD_DIM, tq), F32),
            pltpu.VMEM((tq // 2, tq), F32), pltpu.VMEM((tq // 2, tq), F32),
            pltpu.VMEM((tq // 2, tq), BF16), pltpu.VMEM((tq // 2, tq), BF16),
        ],
        compiler_params=_params("arbitrary", "arbitrary", "arbitrary"),
        name="fox_flash",
    )(q, k, v, c)


def _decode_kernel(pt_ref, q_ref, kn_ref, vn_ref, lfn_ref, *rest, n_grp):
    k_refs, v_refs, lf_refs = rest[:n_grp], rest[n_grp:2 * n_grp], rest[2 * n_grp:3 * n_grp]
    o_ref, acc_ref, m_ref, l_ref, carry_ref = rest[3 * n_grp:]
    p = pl.program_id(1)
    page = k_refs[0].shape[0]
    rows = page * N_HEADS
    grp = rows // LANES

    head_s = lax.broadcasted_iota(jnp.int32, (N_HEADS, LANES), 0)
    lane_h = lax.broadcasted_iota(jnp.int32, (N_HEADS, LANES), 1) % N_HEADS
    own = head_s == lane_h

    def to_pattern(col):
        return jnp.sum(jnp.where(own, col, 0.0), axis=0, keepdims=True)

    def to_column(pat):
        first = lax.broadcasted_iota(jnp.int32, (N_HEADS, LANES), 1) == head_s
        return jnp.sum(jnp.where(first, pat, 0.0), axis=1, keepdims=True)

    @pl.when(p == 0)
    def _():
        kn = kn_ref[...].astype(BF16).astype(F32)
        s_new = jnp.sum(q_ref[...].astype(F32) * kn, axis=-1, keepdims=True)
        m_ref[...] = to_pattern(s_new)
        l_ref[...] = jnp.ones_like(l_ref)
        acc_ref[...] = vn_ref[...].astype(BF16).astype(F32)
        carry_ref[...] = lfn_ref[...] * LOG2E

    a = lax.broadcasted_iota(jnp.int32, (LANES, LANES), 0)
    b = lax.broadcasted_iota(jnp.int32, (LANES, LANES), 1)
    same_head = (a % N_HEADS) == (b % N_HEADS)
    newer_in_row = (same_head & (a // N_HEADS > b // N_HEADS)).astype(BF16)
    all_in_row = same_head.astype(BF16)
    qh = q_ref[...]

    scores, biases = [], []
    run = carry_ref[...]
    for g in range(n_grp):
        lf = lf_refs[g][...] * LOG2E
        row_tot = _dot3(lf, all_in_row)
        in_page = jnp.zeros((1, LANES), F32)
        newer = [None] * grp
        for r in reversed(range(grp)):
            newer[r] = in_page
            in_page = in_page + row_tot[r:r + 1, :]
        biases.append(jnp.concatenate(newer, axis=0) + _dot3(lf, newer_in_row) + run)
        run = run + in_page

        kb = k_refs[g][...].reshape(rows, HEAD_DIM).astype(BF16)
        st = _dot_nt(qh, kb)
        srows = []
        for r in range(grp):
            blk = st[:, r * LANES:(r + 1) * LANES]
            srows.append(jnp.sum(jnp.where(own, blk, 0.0), axis=0, keepdims=True))
        scores.append(jnp.concatenate(srows, axis=0))
    carry_ref[...] = run

    s = jnp.concatenate([scores[g] + biases[g] for g in range(n_grp)], axis=0)
    mx = jnp.max(s, axis=0, keepdims=True)
    for sh in (N_HEADS, 2 * N_HEADS, 4 * N_HEADS):
        mx = jnp.maximum(mx, pltpu.roll(mx, sh, axis=1))
    m_new = jnp.maximum(m_ref[...], mx)
    alpha = jnp.exp2(m_ref[...] - m_new)
    pr = jnp.exp2(s - m_new)
    ps = jnp.sum(pr, axis=0, keepdims=True)
    for sh in (N_HEADS, 2 * N_HEADS, 4 * N_HEADS):
        ps = ps + pltpu.roll(ps, sh, axis=1)
    l_ref[...] = alpha * l_ref[...] + ps
    m_ref[...] = m_new

    pv = jnp.zeros((N_HEADS, HEAD_DIM), F32)
    for g in range(n_grp):
        pcols = [jnp.where(own, jnp.broadcast_to(pr[g * grp + r:g * grp + r + 1, :], (N_HEADS, LANES)), 0.0)
                 for r in range(grp)]
        pmat = jnp.concatenate(pcols, axis=1).astype(BF16)
        pv = pv + _dot(pmat, v_refs[g][...].reshape(rows, HEAD_DIM).astype(BF16))
    acc_ref[...] = to_column(alpha) * acc_ref[...] + pv

    @pl.when(p == pl.num_programs(1) - 1)
    def _():
        o_ref[...] = acc_ref[...] / to_column(l_ref[...])


def _decode_attn(page_table, layer, q3, kn3, vn3, lfn_pat, cache_k, cache_v, cache_lf, n_grp=PAGES_PER_STEP):
    nb, n_pages = page_table.shape
    page = cache_k.shape[2]
    grp = page * N_HEADS // LANES
    last = n_pages - 1
    per_b = lambda b, p, pt: (b, 0, 0)

    def kv_spec(g):
        return pl.BlockSpec((None, None, page, N_HEADS, HEAD_DIM),
                            lambda b, p, pt: (layer, pt[b, last - (p * n_grp + g)], 0, 0, 0))

    def lf_spec(g):
        return pl.BlockSpec((None, None, grp, LANES),
                            lambda b, p, pt: (layer, pt[b, last - (p * n_grp + g)], 0, 0))

    return pl.pallas_call(
        functools.partial(_decode_kernel, n_grp=n_grp),
        out_shape=jax.ShapeDtypeStruct((nb, N_HEADS, HEAD_DIM), F32),
        grid_spec=pltpu.PrefetchScalarGridSpec(
            num_scalar_prefetch=1,
            grid=(nb, n_pages // n_grp),
            in_specs=[
                pl.BlockSpec((None, N_HEADS, HEAD_DIM), per_b),
                pl.BlockSpec((None, N_HEADS, HEAD_DIM), per_b),
                pl.BlockSpec((None, N_HEADS, HEAD_DIM), per_b),
                pl.BlockSpec((None, 1, LANES), per_b),
            ] + [kv_spec(g) for g in range(n_grp)] + [kv_spec(g) for g in range(n_grp)]
              + [lf_spec(g) for g in range(n_grp)],
            out_specs=pl.BlockSpec((None, N_HEADS, HEAD_DIM), per_b),
            scratch_shapes=[
                pltpu.VMEM((N_HEADS, HEAD_DIM), F32),
                pltpu.VMEM((1, LANES), F32),
                pltpu.VMEM((1, LANES), F32),
                pltpu.VMEM((1, LANES), F32),
            ],
        ),
        compiler_params=_params("arbitrary", "arbitrary"),
        name="fox_decode",
    )(page_table, q3, kn3, vn3, lfn_pat, *([cache_k] * n_grp), *([cache_v] * n_grp), *([cache_lf] * n_grp))


def _mm_res_kernel(*refs, has_bias):
    if has_bias:
        x_ref, w_ref, b_ref, res_ref, gate_ref, o_ref, acc_ref = refs
    else:
        x_ref, w_ref, res_ref, gate_ref, o_ref, acc_ref = refs
    kk = pl.program_id(2)

    @pl.when(kk == 0)
    def _():
        acc_ref[...] = jnp.zeros_like(acc_ref)

    acc_ref[...] += _dot(x_ref[...], w_ref[...].astype(BF16))

    @pl.when(kk == pl.num_programs(2) - 1)
    def _():
        acc = acc_ref[...]
        if has_bias:
            acc = acc + b_ref[...]
        o_ref[...] = res_ref[...] + gate_ref[...] * acc


def _mm_res(x, w_all, layer, b, res, mod, gate_k, rows_per_group, tm, tn, tk):
    m, k = x.shape
    n = w_all.shape[2]
    tpg = rows_per_group // tm
    ins = [x, w_all]
    specs = [pl.BlockSpec((tm, tk), lambda i, j, kk: (i, kk)),
             pl.BlockSpec((None, tk, tn), lambda i, j, kk: (layer, kk, j))]
    if b is not None:
        ins.append(b.reshape(1, n))
        specs.append(pl.BlockSpec((1, tn), lambda i, j, kk: (0, j)))
    ins += [res, mod]
    r = mod.shape[1]
    nblk = n // tn
    specs += [pl.BlockSpec((tm, tn), lambda i, j, kk: (i, j)),
              pl.BlockSpec((None, r, tn), lambda i, j, kk: (i // tpg, 0, gate_k * nblk + j))]
    return pl.pallas_call(
        functools.partial(_mm_res_kernel, has_bias=b is not None),
        out_shape=jax.ShapeDtypeStruct((m, n), F32),
        grid=(m // tm, n // tn, k // tk),
        in_specs=specs,
        out_specs=pl.BlockSpec((tm, tn), lambda i, j, kk: (i, j)),
        scratch_shapes=[pltpu.VMEM((tm, tn), F32)],
        compiler_params=_params("arbitrary", "arbitrary", "arbitrary"),
        name="matmul_gated_residual",
    )(*ins)


def _ff1_kernel(x_ref, g_ref, sh_ref, sc_ref, w_ref, o_ref, h_ref):
    @pl.when(pl.program_id(1) == 0)
    def _():
        h_ref[...] = _modnorm(x_ref[...], g_ref[...], sh_ref[...], sc_ref[...]).astype(BF16)

    z = jnp.maximum(_dot(h_ref[...], w_ref[...].astype(BF16)), 0.0)
    o_ref[...] = (z * z).astype(o_ref.dtype)


def _ff1(x, g, mod, rows_per_group, w_all, layer, tm, tn):
    m, d = x.shape
    n = w_all.shape[2]
    tpg = rows_per_group // tm
    return pl.pallas_call(
        _ff1_kernel,
        out_shape=jax.ShapeDtypeStruct((m, n), BF16),
        grid=(m // tm, n // tn),
        in_specs=[
            pl.BlockSpec((tm, d), lambda i, j: (i, 0)),
            pl.BlockSpec((1, d), lambda i, j: (0, 0)),
            _mod_spec(mod, 3, d, tpg),
            _mod_spec(mod, 4, d, tpg),
            pl.BlockSpec((None, d, tn), lambda i, j: (layer, 0, j)),
        ],
        out_specs=pl.BlockSpec((tm, tn), lambda i, j: (i, j)),
        scratch_shapes=[pltpu.VMEM((tm, d), BF16)],
        compiler_params=_params("arbitrary", "arbitrary"),
        name="mlp_up",
    )(x, g.reshape(1, d), mod, mod, w_all)


def _pw1_kernel(x_ref, g_ref, sh_ref, sc_ref, wa_ref, wg_ref, ba_ref, bg_ref, o_ref, h_ref):
    @pl.when(pl.program_id(1) == 0)
    def _():
        h_ref[...] = _modnorm(x_ref[...], g_ref[...], sh_ref[...], sc_ref[...]).astype(BF16)

    hb = h_ref[...]
    a = _dot(hb, wa_ref[...].astype(BF16)) + ba_ref[...]
    gt = _dot(hb, wg_ref[...].astype(BF16)) + bg_ref[...]
    o_ref[...] = a * jax.nn.sigmoid(gt)


def _pw1(x, g, mod, rows_per_group, w_all, b_all, layer, tm, tn):
    m, d = x.shape
    nb = d // tn
    tpg = rows_per_group // tm
    b3 = b_all.reshape(b_all.shape[0], 1, 2 * d)
    return pl.pallas_call(
        _pw1_kernel,
        out_shape=jax.ShapeDtypeStruct((m, d), F32),
        grid=(m // tm, nb),
        in_specs=[
            pl.BlockSpec((tm, d), lambda i, j: (i, 0)),
            pl.BlockSpec((1, d), lambda i, j: (0, 0)),
            _mod_spec(mod, 0, d, tpg),
            _mod_spec(mod, 1, d, tpg),
            pl.BlockSpec((None, d, tn), lambda i, j: (layer, 0, j)),
            pl.BlockSpec((None, d, tn), lambda i, j: (layer, 0, nb + j)),
            pl.BlockSpec((None, 1, tn), lambda i, j: (layer, 0, j)),
            pl.BlockSpec((None, 1, tn), lambda i, j: (layer, 0, nb + j)),
        ],
        out_specs=pl.BlockSpec((tm, tn), lambda i, j: (i, j)),
        scratch_shapes=[pltpu.VMEM((tm, d), BF16)],
        compiler_params=_params("arbitrary", "arbitrary"),
        name="conv_pw1_glu",
    )(x, g.reshape(1, d), mod, mod, w_all, w_all, b3, b3)


def _ln_silu(y, g, b):
    mu = jnp.mean(y, axis=-1, keepdims=True)
    yc = y - mu
    z = yc * lax.rsqrt(jnp.mean(yc * yc, axis=-1, keepdims=True) + EPS) * g + b
    return z * jax.nn.sigmoid(z)


def _conv_kernel(u_ref, w_ref, bdw_ref, g_ref, b_ref, o_ref, win_ref, y_ref, *, tt):
    d = win_ref.shape[1]

    @pl.when(pl.program_id(1) == 0)
    def _():
        win_ref[0:CONV_HALO, :] = jnp.zeros((CONV_HALO, d), F32)

    win_ref[CONV_HALO:CONV_HALO + tt, :] = u_ref[...]
    win_ref[CONV_HALO + tt:, :] = jnp.zeros((SUBLANES, d), F32)
    first = CONV_HALO - (CONV_WIDTH - 1)
    span = CONV_ROWS + SUBLANES

    def chunk(idx, carry):
        r0 = pl.multiple_of((idx // (d // LANES)) * CONV_ROWS, CONV_ROWS)
        c0 = pl.multiple_of((idx % (d // LANES)) * LANES, LANES)
        acc = jnp.zeros((CONV_ROWS, LANES), F32)
        for r in range(SUBLANES):
            z = None
            for k in range(CONV_WIDTH):
                off = first + k
                if off % SUBLANES != r:
                    continue
                term = win_ref[pl.ds(r0 + (off - r), span), pl.ds(c0, LANES)] * w_ref[k:k + 1, pl.ds(c0, LANES)]
                z = term if z is None else z + term
            if r:
                z = pltpu.roll(z, span - r, axis=0)
            acc = acc + z[:CONV_ROWS]
        y_ref[pl.ds(r0, CONV_ROWS), pl.ds(c0, LANES)] = acc
        return carry

    lax.fori_loop(0, (tt // CONV_ROWS) * (d // LANES), chunk, 0)

    nr = 4 * SUBLANES

    def norm(ci, carry):
        r0 = pl.multiple_of(ci * nr, nr)
        y = y_ref[pl.ds(r0, nr), :] + bdw_ref[...]
        o_ref[pl.ds(r0, nr), :] = _ln_silu(y, g_ref[...], b_ref[...]).astype(o_ref.dtype)
        return carry

    lax.fori_loop(0, tt // nr, norm, 0)
    win_ref[0:CONV_HALO, :] = win_ref[tt:tt + CONV_HALO, :]


def _conv_prompt(u, w_dw, b_dw, g_ln, b_ln, tt=512):
    b, t, d = u.shape
    vec = lambda bi, i: (0, 0)
    return pl.pallas_call(
        functools.partial(_conv_kernel, tt=tt),
        out_shape=jax.ShapeDtypeStruct((b, t, d), BF16),
        grid=(b, t // tt),
        in_specs=[
            pl.BlockSpec((None, tt, d), lambda bi, i: (bi, i, 0)),
            pl.BlockSpec((CONV_WIDTH, d), vec),
            pl.BlockSpec((1, d), vec),
            pl.BlockSpec((1, d), vec),
            pl.BlockSpec((1, d), vec),
        ],
        out_specs=pl.BlockSpec((None, tt, d), lambda bi, i: (bi, i, 0)),
        scratch_shapes=[pltpu.VMEM((CONV_HALO + tt + SUBLANES, d), F32), pltpu.VMEM((tt, d), F32)],
        compiler_params=_params("arbitrary", "arbitrary"),
        name="conv_dw_prompt",
    )(u, w_dw, b_dw.reshape(1, d), g_ln.reshape(1, d), b_ln.reshape(1, d))


def _conv_sample_kernel(hist_ref, u_ref, w_ref, bdw_ref, g_ref, b_ref, o_ref):
    nb = hist_ref.shape[0]
    nh = CONV_WIDTH - 1
    rows = []
    for bi in range(nb):
        y = jnp.sum(hist_ref[bi] * w_ref[0:nh, :], axis=0, keepdims=True)
        rows.append(y + u_ref[bi:bi + 1, :] * w_ref[nh:nh + 1, :] + bdw_ref[...])
    y = jnp.concatenate(rows, axis=0)
    z = _ln_silu(y, g_ref[...], b_ref[...])
    o_ref[...] = jnp.concatenate([z, jnp.zeros((o_ref.shape[0] - nb, z.shape[1]), F32)], axis=0).astype(o_ref.dtype)


def _conv_sample(hist, u, w_dw, b_dw, g_ln, b_ln):
    d = u.shape[1]
    return pl.pallas_call(
        _conv_sample_kernel,
        out_shape=jax.ShapeDtypeStruct((SAMPLE_ROWS, d), BF16),
        compiler_params=pltpu.CompilerParams(vmem_limit_bytes=VMEM_LIMIT),
        name="conv_dw_sample",
    )(hist, u, w_dw, b_dw.reshape(1, d), g_ln.reshape(1, d), b_ln.reshape(1, d))


def kernel(x_prompt, x_sample, cache_k, cache_v, cache_logf, state_conv, page_table, c_prompt, c_sample,
           w_mod, b_mod, g_mix, g_ffn, w_qkvf, b_f, g_q, g_k, w_o,
           w_pw1, b_pw1, w_dw, b_dw, g_cln, b_cln, w_pw2, b_pw2, w_ff1, w_ff2):
    nbp, t, d = x_prompt.shape
    nbs = x_sample.shape[0]
    depth = w_mod.shape[0]
    n_fox = cache_k.shape[0]
    mp = nbp * t
    pool, page = cache_k.shape[1], cache_k.shape[2]
    assert x_sample.shape[1] == 1 and nbp + nbs <= SAMPLE_ROWS
    assert LANES % N_HEADS == 0 and (page * N_HEADS) % LANES == 0

    c16 = jnp.zeros((SAMPLE_ROWS, d), F32).at[:nbp].set(c_prompt).at[nbp:nbp + nbs].set(c_sample)
    mods = _mods(c16, w_mod, b_mod)

    xp = x_prompt.reshape(mp, d)
    xs = jnp.zeros((SAMPLE_ROWS, d), F32).at[:nbs].set(x_sample.reshape(nbs, d))

    wf_pad = jnp.zeros((n_fox, d, LANES), F32).at[:, :, :N_HEADS].set(w_qkvf[:, :, 3 * d:])
    bf_pad = jnp.zeros((n_fox, 1, LANES), F32).at[:, 0, :N_HEADS].set(b_f)
    cache_lf = cache_logf.reshape(n_fox, pool, page * N_HEADS // LANES, LANES)

    kp_l, vp_l, lfp_l, cvp_l, ks_l, vs_l, lfs_l, cvs_l = [], [], [], [], [], [], [], []
    tm_p, tm_s = 1024, SAMPLE_ROWS
    for i in range(depth):
        j = i // 2
        modp = mods[i, :nbp].reshape(nbp, 1, 6 * d)
        mods_s = jnp.zeros((1, SAMPLE_ROWS, 6 * d), F32).at[0, :nbs].set(mods[i, nbp:nbp + nbs])
        if i % 2 == 0:
            q, k32, kb, v32, vb, lf = _qkvf(xp, g_mix[i], modp, t, w_qkvf, j, wf_pad[j], bf_pad[j], g_q[j], g_k[j],
                                            tm_p, 256)
            lf_p = lf[:, :N_HEADS].reshape(nbp, t, N_HEADS)
            c = _cumsum_rows(jnp.transpose(lf_p, (0, 2, 1)).reshape(nbp * N_HEADS, t))
            o = _flash(q.reshape(nbp, t, d), kb.reshape(nbp, t, d), vb.reshape(nbp, t, d),
                       c.reshape(nbp, N_HEADS, 1, t))
            xp = _mm_res(o.reshape(mp, d), w_o, j, None, xp, modp, 2, t, tm_p, 1024, 1024)
            kp_l.append(k32.reshape(nbp, t, N_HEADS, HEAD_DIM))
            vp_l.append(v32.reshape(nbp, t, N_HEADS, HEAD_DIM))
            lfp_l.append(lf_p)
            q, k32, kb, v32, vb, lf = _qkvf(xs, g_mix[i], mods_s, SAMPLE_ROWS, w_qkvf, j, wf_pad[j], bf_pad[j],
                                            g_q[j], g_k[j], tm_s, 512)
            lf_s = lf[:nbs, :N_HEADS]
            o = _decode_attn(page_table, j,
                             q[:nbs].reshape(nbs, N_HEADS, HEAD_DIM),
                             k32[:nbs].reshape(nbs, N_HEADS, HEAD_DIM),
                             v32[:nbs].reshape(nbs, N_HEADS, HEAD_DIM),
                             jnp.tile(lf_s, (1, LANES // N_HEADS)).reshape(nbs, 1, LANES),
                             cache_k, cache_v, cache_lf)
            o16 = jnp.zeros((SAMPLE_ROWS, d), BF16).at[:nbs].set(o.reshape(nbs, d).astype(BF16))
            xs = _mm_res(o16, w_o, j, None, xs, mods_s, 2, SAMPLE_ROWS, tm_s, 1024, d)
            ks_l.append(k32[:nbs].reshape(nbs, 1, N_HEADS, HEAD_DIM))
            vs_l.append(v32[:nbs].reshape(nbs, 1, N_HEADS, HEAD_DIM))
            lfs_l.append(lf_s.reshape(nbs, 1, N_HEADS))
        else:
            u = _pw1(xp, g_mix[i], modp, t, w_pw1, b_pw1, j, tm_p, 512)
            u3 = u.reshape(nbp, t, d)
            act = _conv_prompt(u3, w_dw[j], b_dw[j], g_cln[j], b_cln[j])
            xp = _mm_res(act.reshape(mp, d), w_pw2, j, b_pw2[j], xp, modp, 2, t, tm_p, 1024, 1024)
            cvp_l.append(u3[:, t - (CONV_WIDTH - 1):])
            u = _pw1(xs, g_mix[i], mods_s, SAMPLE_ROWS, w_pw1, b_pw1, j, tm_s, 1024)
            act = _conv_sample(state_conv[j], u, w_dw[j], b_dw[j], g_cln[j], b_cln[j])
            xs = _mm_res(act, w_pw2, j, b_pw2[j], xs, mods_s, 2, SAMPLE_ROWS, tm_s, 1024, d)
            cvs_l.append(jnp.concatenate([state_conv[j][:, 1:], u[:nbs, None, :]], axis=1))
        hid = _ff1(xp, g_ffn[i], modp, t, w_ff1, i, tm_p, 512)
        xp = _mm_res(hid, w_ff2, i, None, xp, modp, 5, t, tm_p, 1024, 1024)
        hid = _ff1(xs, g_ffn[i], mods_s, SAMPLE_ROWS, w_ff1, i, tm_s, 1024)
        xs = _mm_res(hid, w_ff2, i, None, xs, mods_s, 5, SAMPLE_ROWS, tm_s, 1024, 2048)

    return (xp.reshape(nbp, t, d), xs[:nbs].reshape(nbs, 1, d),
            jnp.stack(kp_l), jnp.stack(vp_l), jnp.stack(lfp_l), jnp.stack(cvp_l),
            jnp.stack(ks_l), jnp.stack(vs_l), jnp.stack(lfs_l), jnp.stack(cvs_l))
```

```python
import functools
import math

import jax
import jax.numpy as jnp
from jax import lax
from jax.experimental import pallas as pl
from jax.experimental.pallas import tpu as pltpu

F32 = jnp.float32
BF16 = jnp.bfloat16

N_HEADS = 16
HEAD_DIM = 128
CONV_WIDTH = 31
EPS = 1e-6
NEG_INF = -1e30
LOG2E = math.log2(math.e)
LANES = 128
SUBLANES = 8
VMEM_LIMIT = 56 * 1024 * 1024
SAMPLE_ROWS = 16
CONV_HALO = 32
CONV_ROWS = 64
PAGES_PER_STEP = 8


def _params(*sem):
    return pltpu.CompilerParams(dimension_semantics=sem, vmem_limit_bytes=VMEM_LIMIT)


def _dot(a, b):
    return jnp.dot(a, b, preferred_element_type=F32)


def _dot_nt(a, b):
    return lax.dot_general(a, b, (((1,), (1,)), ((), ())), preferred_element_type=F32)


def _split3(x):
    hi = x.astype(BF16)
    r1 = x - hi.astype(F32)
    mid = r1.astype(BF16)
    lo = (r1 - mid.astype(F32)).astype(BF16)
    return hi, mid, lo


def _dot3(x, m):
    hi, mid, lo = _split3(x)
    return _dot(hi, m) + _dot(mid, m) + _dot(lo, m)


def _modnorm(x, g, shift, scale):
    ms = jnp.mean(x * x, axis=-1, keepdims=True)
    y = x * lax.rsqrt(ms + EPS) * g
    return y * (1.0 + scale) + shift


def _log_sigmoid(z):
    return jnp.minimum(z, 0.0) - jnp.log1p(jnp.exp(-jnp.abs(z)))


def _mod_spec(mod, k, d, tiles_per_group):
    r = mod.shape[1]
    return pl.BlockSpec((None, r, d), lambda i, *_: (i // tiles_per_group, 0, k))


def _mods_kernel(c_ref, w_ref, b_ref, o_ref):
    c = c_ref[...]
    a = (c * jax.nn.sigmoid(c)).astype(BF16)
    o_ref[...] = _dot(a, w_ref[...].astype(BF16)) + b_ref[...]


def _mods(c16, w_mod, b_mod, tn=1024):
    depth, d, n = w_mod.shape
    return pl.pallas_call(
        _mods_kernel,
        out_shape=jax.ShapeDtypeStruct((depth, SAMPLE_ROWS, n), F32),
        grid=(depth, n // tn),
        in_specs=[
            pl.BlockSpec((SAMPLE_ROWS, d), lambda l, j: (0, 0)),
            pl.BlockSpec((None, d, tn), lambda l, j: (l, 0, j)),
            pl.BlockSpec((None, 1, tn), lambda l, j: (l, 0, j)),
        ],
        out_specs=pl.BlockSpec((None, SAMPLE_ROWS, tn), lambda l, j: (l, 0, j)),
        compiler_params=_params("arbitrary", "arbitrary"),
        name="adaln_mods",
    )(c16, w_mod, b_mod.reshape(depth, 1, n))


def _qkvf_kernel(x_ref, g_ref, sh_ref, sc_ref, wq_ref, wk_ref, wv_ref, wf_ref, bf_ref, gq_ref, gk_ref,
                 q_ref, k32_ref, kb_ref, v32_ref, vb_ref, lf_ref, h_ref, *, tn):
    @pl.when(pl.program_id(1) == 0)
    def _():
        hb = _modnorm(x_ref[...], g_ref[...], sh_ref[...], sc_ref[...]).astype(BF16)
        h_ref[...] = hb
        lf_ref[...] = _log_sigmoid(_dot(hb, wf_ref[...].astype(BF16)) + bf_ref[...])

    hb = h_ref[...]
    zq = _dot(hb, wq_ref[...].astype(BF16))
    zk = _dot(hb, wk_ref[...].astype(BF16))
    zv = _dot(hb, wv_ref[...].astype(BF16))
    q_scale = HEAD_DIM ** -0.5 * LOG2E
    for hh in range(tn // HEAD_DIM):
        sl = slice(hh * HEAD_DIM, (hh + 1) * HEAD_DIM)
        zh = zq[:, sl]
        qn = zh * lax.rsqrt(jnp.mean(zh * zh, axis=-1, keepdims=True) + EPS) * gq_ref[...]
        q_ref[:, sl] = (qn * q_scale).astype(BF16)
        zh = zk[:, sl]
        kn = zh * lax.rsqrt(jnp.mean(zh * zh, axis=-1, keepdims=True) + EPS) * gk_ref[...]
        k32_ref[:, sl] = kn
        kb_ref[:, sl] = kn.astype(BF16)
    v32_ref[...] = zv
    vb_ref[...] = zv.astype(BF16)


def _qkvf(x, g, mod, rows_per_group, w_all, layer, wf_pad, bf_pad, g_q, g_k, tm, tn):
    m, d = x.shape
    nb = d // tn
    tpg = rows_per_group // tm
    row = lambda i, j: (i, j)
    vec = lambda i, j: (0, 0)
    return pl.pallas_call(
        functools.partial(_qkvf_kernel, tn=tn),
        out_shape=(
            jax.ShapeDtypeStruct((m, d), BF16),
            jax.ShapeDtypeStruct((m, d), F32),
            jax.ShapeDtypeStruct((m, d), BF16),
            jax.ShapeDtypeStruct((m, d), F32),
            jax.ShapeDtypeStruct((m, d), BF16),
            jax.ShapeDtypeStruct((m, LANES), F32),
        ),
        grid=(m // tm, nb),
        in_specs=[
            pl.BlockSpec((tm, d), lambda i, j: (i, 0)),
            pl.BlockSpec((1, d), vec),
            _mod_spec(mod, 0, d, tpg),
            _mod_spec(mod, 1, d, tpg),
            pl.BlockSpec((None, d, tn), lambda i, j: (layer, 0, j)),
            pl.BlockSpec((None, d, tn), lambda i, j: (layer, 0, nb + j)),
            pl.BlockSpec((None, d, tn), lambda i, j: (layer, 0, 2 * nb + j)),
            pl.BlockSpec((d, LANES), vec),
            pl.BlockSpec((1, LANES), vec),
            pl.BlockSpec((1, HEAD_DIM), vec),
            pl.BlockSpec((1, HEAD_DIM), vec),
        ],
        out_specs=(
            pl.BlockSpec((tm, tn), row), pl.BlockSpec((tm, tn), row), pl.BlockSpec((tm, tn), row),
            pl.BlockSpec((tm, tn), row), pl.BlockSpec((tm, tn), row),
            pl.BlockSpec((tm, LANES), lambda i, j: (i, 0)),
        ),
        scratch_shapes=[pltpu.VMEM((tm, d), BF16)],
        compiler_params=_params("arbitrary", "arbitrary"),
        name="fox_qkvf",
    )(x, g.reshape(1, d), mod, mod, w_all, w_all, w_all, wf_pad, bf_pad, g_q.reshape(1, -1), g_k.reshape(1, -1))


def _cumsum_kernel(x_ref, o_ref):
    n = x_ref.shape[1]
    j = lax.broadcasted_iota(jnp.int32, (LANES, LANES), 0)
    s = lax.broadcasted_iota(jnp.int32, (LANES, LANES), 1)
    tri = (j <= s).astype(BF16)
    carry = jnp.zeros((x_ref.shape[0], 1), F32)
    for c in range(n // LANES):
        sl = slice(c * LANES, (c + 1) * LANES)
        cs = _dot3(x_ref[:, sl], tri)
        o_ref[:, sl] = cs + carry
        carry = carry + cs[:, LANES - 1:LANES]


def _cumsum_rows(x):
    return pl.pallas_call(
        _cumsum_kernel,
        out_shape=jax.ShapeDtypeStruct(x.shape, F32),
        name="logf_cumsum",
    )(x)


def _flash_kernel(q_ref, k_ref, v_ref, c_ref, o_ref, kaug_ref, caq_ref, vt_ref, acc_ref,
                  sa_ref, sb_ref, pa_ref, pb_ref, *, tq):
    i = pl.program_id(2)
    t = k_ref.shape[0]

    @pl.when(i == 0)
    def _():
        lane = lax.broadcasted_iota(jnp.int32, (LANES, LANES), 1)

        def build(ci, carry):
            r0 = pl.multiple_of(ci * LANES, LANES)
            crow = c_ref[:, pl.ds(r0, LANES)] * LOG2E
            col = jnp.broadcast_to(crow, (LANES, LANES)).T
            hi, mid, lo = [p.astype(F32) for p in _split3(col)]
            pieces_k = jnp.where(lane == 0, hi, jnp.where(lane == 1, mid, jnp.where(lane == 2, lo, 0.0)))
            ones_k = jnp.where((lane >= 3) & (lane < 6), 1.0, 0.0)
            pieces_q = jnp.where(lane == 3, hi, jnp.where(lane == 4, mid, jnp.where(lane == 5, lo, 0.0)))
            ones_q = jnp.where(lane < 3, -1.0, 0.0)
            kaug_ref[pl.ds(r0, LANES), 0:HEAD_DIM] = k_ref[pl.ds(r0, LANES), :]
            kaug_ref[pl.ds(r0, LANES), HEAD_DIM:2 * HEAD_DIM] = (pieces_k + ones_k).astype(BF16)
            caq_ref[pl.ds(r0, LANES), :] = (pieces_q + ones_q).astype(BF16)
            vt_ref[:, pl.ds(r0, LANES)] = v_ref[pl.ds(r0, LANES), :].astype(F32).T.astype(BF16)
            return carry

        lax.fori_loop(0, t // LANES, build, 0)

    q0 = pl.multiple_of(i * tq, tq)
    q_aug = jnp.concatenate([q_ref[...], caq_ref[pl.ds(q0, tq), :]], axis=1)
    tk = tq // 2
    acc_ref[...] = jnp.zeros_like(acc_ref)
    pb_ref[...] = jnp.zeros_like(pb_ref)

    def qk(k0):
        return _dot_nt(kaug_ref[pl.ds(k0, tk), :], q_aug)

    def soft(s, m, l, mask_from):
        if mask_from is not None:
            kk = lax.broadcasted_iota(jnp.int32, (tk, tq), 0) + mask_from
            qq = lax.broadcasted_iota(jnp.int32, (tk, tq), 1)
            s = jnp.where(kk <= qq, s, NEG_INF)
        m_new = jnp.maximum(m, jnp.max(s, axis=0, keepdims=True))
        alpha = jnp.exp2(m - m_new)
        p = jnp.exp2(s - m_new)
        return m_new, alpha * l + jnp.sum(p, axis=0, keepdims=True), alpha, p.astype(BF16)

    def pv(alpha, k0, p):
        acc_ref[...] = alpha * acc_ref[...] + _dot(vt_ref[:, pl.ds(k0, tk)], p)

    def pair(jj, carry):
        m, l, alpha = carry
        ka = pl.multiple_of(jj * tq, tq)
        sb_ref[...] = qk(ka + tk)
        pv(alpha, pl.multiple_of(jnp.maximum(ka - tk, 0), tk), pb_ref[...])
        m, l, alpha, p = soft(sa_ref[...], m, l, None)
        pa_ref[...] = p
        sa_ref[...] = qk(ka + tq)
        pv(alpha, ka, pa_ref[...])
        m, l, alpha, p = soft(sb_ref[...], m, l, None)
        pb_ref[...] = p
        return m, l, alpha

    sa_ref[...] = qk(0)
    m0 = jnp.full((1, tq), NEG_INF, F32)
    l0 = jnp.zeros((1, tq), F32)
    m, l, alpha = lax.fori_loop(0, i, pair, (m0, l0, jnp.ones((1, tq), F32)))
    sb_ref[...] = qk(q0 + tk)
    pv(alpha, pl.multiple_of(jnp.maximum(q0 - tk, 0), tk), pb_ref[...])
    m, l, alpha, p = soft(sa_ref[...], m, l, 0)
    pv(alpha, q0, p)
    m, l, alpha, p = soft(sb_ref[...], m, l, tk)
    pv(alpha, q0 + tk, p)
    ot = acc_ref[...] / l
    for c in range(tq // LANES):
        sl = slice(c * LANES, (c + 1) * LANES)
        o_ref[sl, :] = ot[:, sl].T.astype(o_ref.dtype)


def _flash(q, k, v, c, tq=512):
    b, t, d = q.shape
    h = d // HEAD_DIM
    return pl.pallas_call(
        functools.partial(_flash_kernel, tq=tq),
        out_shape=jax.ShapeDtypeStruct((b, t, d), BF16),
        grid=(b, h, t // tq),
        in_specs=[
            pl.BlockSpec((None, tq, HEAD_DIM), lambda bi, hi, i: (bi, i, hi)),
            pl.BlockSpec((None, t, HEAD_DIM), lambda bi, hi, i: (bi, 0, hi)),
            pl.BlockSpec((None, t, HEAD_DIM), lambda bi, hi, i: (bi, 0, hi)),
            pl.BlockSpec((None, None, 1, t), lambda bi, hi, i: (bi, hi, 0, 0)),
        ],
        out_specs=pl.BlockSpec((None, tq, HEAD_DIM), lambda bi, hi, i: (bi, i, hi)),
        scratch_shapes=[
            pltpu.VMEM((t, 2 * HEAD_DIM), BF16),
            pltpu.VMEM((t, HEAD_DIM), BF16),
            pltpu.VMEM((HEAD_DIM, t), BF16),
            pltpu.VMEM((HEAD_DIM, tq), F32),
            pltpu.VMEM((tq // 2, tq), F32), pltpu.VMEM((tq // 2, tq), F32),
            pltpu.VMEM((tq // 2, tq), BF16), pltpu.VMEM((tq // 2, tq), BF16),
        ],
        compiler_params=_params("arbitrary", "arbitrary", "arbitrary"),
        name="fox_flash",
    )(q, k, v, c)


def _decode_kernel(pt_ref, q_ref, kn_ref, vn_ref, lfn_ref, *rest, n_grp):
    k_refs, v_refs, lf_refs = rest[:n_grp], rest[n_grp:2 * n_grp], rest[2 * n_grp:3 * n_grp]
    o_ref, acc_ref, m_ref, l_ref, carry_ref = rest[3 * n_grp:]
    p = pl.program_id(1)
    page = k_refs[0].shape[0]
    rows = page * N_HEADS
    grp = rows // LANES

    head_s = lax.broadcasted_iota(jnp.int32, (N_HEADS, LANES), 0)
    lane_h = lax.broadcasted_iota(jnp.int32, (N_HEADS, LANES), 1) % N_HEADS
    own = head_s == lane_h

    def to_pattern(col):
        return jnp.sum(jnp.where(own, col, 0.0), axis=0, keepdims=True)

    def to_column(pat):
        first = lax.broadcasted_iota(jnp.int32, (N_HEADS, LANES), 1) == head_s
        return jnp.sum(jnp.where(first, pat, 0.0), axis=1, keepdims=True)

    @pl.when(p == 0)
    def _():
        kn = kn_ref[...].astype(BF16).astype(F32)
        s_new = jnp.sum(q_ref[...].astype(F32) * kn, axis=-1, keepdims=True)
        m_ref[...] = to_pattern(s_new)
        l_ref[...] = jnp.ones_like(l_ref)
        acc_ref[...] = vn_ref[...].astype(BF16).astype(F32)
        carry_ref[...] = lfn_ref[...] * LOG2E

    a = lax.broadcasted_iota(jnp.int32, (LANES, LANES), 0)
    b = lax.broadcasted_iota(jnp.int32, (LANES, LANES), 1)
    same_head = (a % N_HEADS) == (b % N_HEADS)
    newer_in_row = (same_head & (a // N_HEADS > b // N_HEADS)).astype(BF16)
    all_in_row = same_head.astype(BF16)
    qh = q_ref[...]

    scores, biases = [], []
    run = carry_ref[...]
    for g in range(n_grp):
        lf = lf_refs[g][...] * LOG2E
        row_tot = _dot3(lf, all_in_row)
        in_page = jnp.zeros((1, LANES), F32)
        newer = [None] * grp
        for r in reversed(range(grp)):
            newer[r] = in_page
            in_page = in_page + row_tot[r:r + 1, :]
        biases.append(jnp.concatenate(newer, axis=0) + _dot3(lf, newer_in_row) + run)
        run = run + in_page

        kb = k_refs[g][...].reshape(rows, HEAD_DIM).astype(BF16)
        st = _dot_nt(qh, kb)
        srows = []
        for r in range(grp):
            blk = st[:, r * LANES:(r + 1) * LANES]
            srows.append(jnp.sum(jnp.where(own, blk, 0.0), axis=0, keepdims=True))
        scores.append(jnp.concatenate(srows, axis=0))
    carry_ref[...] = run

    s = jnp.concatenate([scores[g] + biases[g] for g in range(n_grp)], axis=0)
    mx = jnp.max(s, axis=0, keepdims=True)
    for sh in (N_HEADS, 2 * N_HEADS, 4 * N_HEADS):
        mx = jnp.maximum(mx, pltpu.roll(mx, sh, axis=1))
    m_new = jnp.maximum(m_ref[...], mx)
    alpha = jnp.exp2(m_ref[...] - m_new)
    pr = jnp.exp2(s - m_new)
    ps = jnp.sum(pr, axis=0, keepdims=True)
    for sh in (N_HEADS, 2 * N_HEADS, 4 * N_HEADS):
        ps = ps + pltpu.roll(ps, sh, axis=1)
    l_ref[...] = alpha * l_ref[...] + ps
    m_ref[...] = m_new

    pv = jnp.zeros((N_HEADS, HEAD_DIM), F32)
    for g in range(n_grp):
        pcols = [jnp.where(own, jnp.broadcast_to(pr[g * grp + r:g * grp + r + 1, :], (N_HEADS, LANES)), 0.0)
                 for r in range(grp)]
        pmat = jnp.concatenate(pcols, axis=1).astype(BF16)
        pv = pv + _dot(pmat, v_refs[g][...].reshape(rows, HEAD_DIM).astype(BF16))
    acc_ref[...] = to_column(alpha) * acc_ref[...] + pv

    @pl.when(p == pl.num_programs(1) - 1)
    def _():
        o_ref[...] = acc_ref[...] / to_column(l_ref[...])


def _decode_attn(page_table, layer, q3, kn3, vn3, lfn_pat, cache_k, cache_v, cache_lf, n_grp=PAGES_PER_STEP):
    nb, n_pages = page_table.shape
    page = cache_k.shape[2]
    grp = page * N_HEADS // LANES
    last = n_pages - 1
    per_b = lambda b, p, pt: (b, 0, 0)

    def kv_spec(g):
        return pl.BlockSpec((None, None, page, N_HEADS, HEAD_DIM),
                            lambda b, p, pt: (layer, pt[b, last - (p * n_grp + g)], 0, 0, 0))

    def lf_spec(g):
        return pl.BlockSpec((None, None, grp, LANES),
                            lambda b, p, pt: (layer, pt[b, last - (p * n_grp + g)], 0, 0))

    return pl.pallas_call(
        functools.partial(_decode_kernel, n_grp=n_grp),
        out_shape=jax.ShapeDtypeStruct((nb, N_HEADS, HEAD_DIM), F32),
        grid_spec=pltpu.PrefetchScalarGridSpec(
            num_scalar_prefetch=1,
            grid=(nb, n_pages // n_grp),
            in_specs=[
                pl.BlockSpec((None, N_HEADS, HEAD_DIM), per_b),
                pl.BlockSpec((None, N_HEADS, HEAD_DIM), per_b),
                pl.BlockSpec((None, N_HEADS, HEAD_DIM), per_b),
                pl.BlockSpec((None, 1, LANES), per_b),
            ] + [kv_spec(g) for g in range(n_grp)] + [kv_spec(g) for g in range(n_grp)]
              + [lf_spec(g) for g in range(n_grp)],
            out_specs=pl.BlockSpec((None, N_HEADS, HEAD_DIM), per_b),
            scratch_shapes=[
                pltpu.VMEM((N_HEADS, HEAD_DIM), F32),
                pltpu.VMEM((1, LANES), F32),
                pltpu.VMEM((1, LANES), F32),
                pltpu.VMEM((1, LANES), F32),
            ],
        ),
        compiler_params=_params("arbitrary", "arbitrary"),
        name="fox_decode",
    )(page_table, q3, kn3, vn3, lfn_pat, *([cache_k] * n_grp), *([cache_v] * n_grp), *([cache_lf] * n_grp))


def _mm_res_kernel(*refs, has_bias, single_k):
    if has_bias:
        x_ref, w_ref, b_ref, res_ref, gate_ref, o_ref, acc_ref = refs
    else:
        x_ref, w_ref, res_ref, gate_ref, o_ref, acc_ref = refs
    if single_k:
        acc = _dot(x_ref[...], w_ref[...].astype(BF16))
        if has_bias:
            acc = acc + b_ref[...]
        o_ref[...] = res_ref[...] + gate_ref[...] * acc
        return
    kk = pl.program_id(2)

    @pl.when(kk == 0)
    def _():
        acc_ref[...] = jnp.zeros_like(acc_ref)

    acc_ref[...] += _dot(x_ref[...], w_ref[...].astype(BF16))

    @pl.when(kk == pl.num_programs(2) - 1)
    def _():
        acc = acc_ref[...]
        if has_bias:
            acc = acc + b_ref[...]
        o_ref[...] = res_ref[...] + gate_ref[...] * acc


def _mm_res(x, w_all, layer, b, res, mod, gate_k, rows_per_group, tm, tn, tk):
    m, k = x.shape
    n = w_all.shape[2]
    tpg = rows_per_group // tm
    ins = [x, w_all]
    specs = [pl.BlockSpec((tm, tk), lambda i, j, kk: (i, kk)),
             pl.BlockSpec((None, tk, tn), lambda i, j, kk: (layer, kk, j))]
    if b is not None:
        ins.append(b.reshape(1, n))
        specs.append(pl.BlockSpec((1, tn), lambda i, j, kk: (0, j)))
    ins += [res, mod]
    r = mod.shape[1]
    nblk = n // tn
    specs += [pl.BlockSpec((tm, tn), lambda i, j, kk: (i, j)),
              pl.BlockSpec((None, r, tn), lambda i, j, kk: (i // tpg, 0, gate_k * nblk + j))]
    return pl.pallas_call(
        functools.partial(_mm_res_kernel, has_bias=b is not None, single_k=(k == tk)),
        out_shape=jax.ShapeDtypeStruct((m, n), F32),
        grid=(m // tm, n // tn, k // tk),
        in_specs=specs,
        out_specs=pl.BlockSpec((tm, tn), lambda i, j, kk: (i, j)),
        scratch_shapes=[pltpu.VMEM((tm, tn), F32)],
        compiler_params=_params("arbitrary", "arbitrary", "arbitrary"),
        name="matmul_gated_residual",
    )(*ins)


def _ff1_kernel(x_ref, g_ref, sh_ref, sc_ref, w_ref, o_ref, h_ref):
    @pl.when(pl.program_id(1) == 0)
    def _():
        h_ref[...] = _modnorm(x_ref[...], g_ref[...], sh_ref[...], sc_ref[...]).astype(BF16)

    z = jnp.maximum(_dot(h_ref[...], w_ref[...].astype(BF16)), 0.0)
    o_ref[...] = (z * z).astype(o_ref.dtype)


def _ff1(x, g, mod, rows_per_group, w_all, layer, tm, tn):
    m, d = x.shape
    n = w_all.shape[2]
    tpg = rows_per_group // tm
    return pl.pallas_call(
        _ff1_kernel,
        out_shape=jax.ShapeDtypeStruct((m, n), BF16),
        grid=(m // tm, n // tn),
        in_specs=[
            pl.BlockSpec((tm, d), lambda i, j: (i, 0)),
            pl.BlockSpec((1, d), lambda i, j: (0, 0)),
            _mod_spec(mod, 3, d, tpg),
            _mod_spec(mod, 4, d, tpg),
            pl.BlockSpec((None, d, tn), lambda i, j: (layer, 0, j)),
        ],
        out_specs=pl.BlockSpec((tm, tn), lambda i, j: (i, j)),
        scratch_shapes=[pltpu.VMEM((tm, d), BF16)],
        compiler_params=_params("arbitrary", "arbitrary"),
        name="mlp_up",
    )(x, g.reshape(1, d), mod, mod, w_all)


def _pw1_kernel(x_ref, g_ref, sh_ref, sc_ref, wa_ref, wg_ref, ba_ref, bg_ref, o_ref, h_ref):
    @pl.when(pl.program_id(1) == 0)
    def _():
        h_ref[...] = _modnorm(x_ref[...], g_ref[...], sh_ref[...], sc_ref[...]).astype(BF16)

    hb = h_ref[...]
    a = _dot(hb, wa_ref[...].astype(BF16)) + ba_ref[...]
    gt = _dot(hb, wg_ref[...].astype(BF16)) + bg_ref[...]
    o_ref[...] = a * jax.nn.sigmoid(gt)


def _pw1(x, g, mod, rows_per_group, w_all, b_all, layer, tm, tn):
    m, d = x.shape
    nb = d // tn
    tpg = rows_per_group // tm
    b3 = b_all.reshape(b_all.shape[0], 1, 2 * d)
    return pl.pallas_call(
        _pw1_kernel,
        out_shape=jax.ShapeDtypeStruct((m, d), F32),
        grid=(m // tm, nb),
        in_specs=[
            pl.BlockSpec((tm, d), lambda i, j: (i, 0)),
            pl.BlockSpec((1, d), lambda i, j: (0, 0)),
            _mod_spec(mod, 0, d, tpg),
            _mod_spec(mod, 1, d, tpg),
            pl.BlockSpec((None, d, tn), lambda i, j: (layer, 0, j)),
            pl.BlockSpec((None, d, tn), lambda i, j: (layer, 0, nb + j)),
            pl.BlockSpec((None, 1, tn), lambda i, j: (layer, 0, j)),
            pl.BlockSpec((None, 1, tn), lambda i, j: (layer, 0, nb + j)),
        ],
        out_specs=pl.BlockSpec((tm, tn), lambda i, j: (i, j)),
        scratch_shapes=[pltpu.VMEM((tm, d), BF16)],
        compiler_params=_params("arbitrary", "arbitrary"),
        name="conv_pw1_glu",
    )(x, g.reshape(1, d), mod, mod, w_all, w_all, b3, b3)


def _ln_silu(y, g, b):
    mu = jnp.mean(y, axis=-1, keepdims=True)
    yc = y - mu
    z = yc * lax.rsqrt(jnp.mean(yc * yc, axis=-1, keepdims=True) + EPS) * g + b
    return z * jax.nn.sigmoid(z)


def _conv_kernel(u_ref, w_ref, bdw_ref, g_ref, b_ref, o_ref, win_ref, y_ref, *, tt):
    d = win_ref.shape[1]

    @pl.when(pl.program_id(1) == 0)
    def _():
        win_ref[0:CONV_HALO, :] = jnp.zeros((CONV_HALO, d), F32)

    win_ref[CONV_HALO:CONV_HALO + tt, :] = u_ref[...]
    win_ref[CONV_HALO + tt:, :] = jnp.zeros((SUBLANES, d), F32)
    first = CONV_HALO - (CONV_WIDTH - 1)
    span = CONV_ROWS + SUBLANES

    def chunk(idx, carry):
        r0 = pl.multiple_of((idx // (d // LANES)) * CONV_ROWS, CONV_ROWS)
        c0 = pl.multiple_of((idx % (d // LANES)) * LANES, LANES)
        acc = jnp.zeros((CONV_ROWS, LANES), F32)
        for r in range(SUBLANES):
            z = None
            for k in range(CONV_WIDTH):
                off = first + k
                if off % SUBLANES != r:
                    continue
                term = win_ref[pl.ds(r0 + (off - r), span), pl.ds(c0, LANES)] * w_ref[k:k + 1, pl.ds(c0, LANES)]
                z = term if z is None else z + term
            if r:
                z = pltpu.roll(z, span - r, axis=0)
            acc = acc + z[:CONV_ROWS]
        y_ref[pl.ds(r0, CONV_ROWS), pl.ds(c0, LANES)] = acc
        return carry

    lax.fori_loop(0, (tt // CONV_ROWS) * (d // LANES), chunk, 0)

    nr = 4 * SUBLANES

    def norm(ci, carry):
        r0 = pl.multiple_of(ci * nr, nr)
        y = y_ref[pl.ds(r0, nr), :] + bdw_ref[...]
        o_ref[pl.ds(r0, nr), :] = _ln_silu(y, g_ref[...], b_ref[...]).astype(o_ref.dtype)
        return carry

    lax.fori_loop(0, tt // nr, norm, 0)
    win_ref[0:CONV_HALO, :] = win_ref[tt:tt + CONV_HALO, :]


def _conv_prompt(u, w_dw, b_dw, g_ln, b_ln, tt=512):
    b, t, d = u.shape
    vec = lambda bi, i: (0, 0)
    return pl.pallas_call(
        functools.partial(_conv_kernel, tt=tt),
        out_shape=jax.ShapeDtypeStruct((b, t, d), BF16),
        grid=(b, t // tt),
        in_specs=[
            pl.BlockSpec((None, tt, d), lambda bi, i: (bi, i, 0)),
            pl.BlockSpec((CONV_WIDTH, d), vec),
            pl.BlockSpec((1, d), vec),
            pl.BlockSpec((1, d), vec),
            pl.BlockSpec((1, d), vec),
        ],
        out_specs=pl.BlockSpec((None, tt, d), lambda bi, i: (bi, i, 0)),
        scratch_shapes=[pltpu.VMEM((CONV_HALO + tt + SUBLANES, d), F32), pltpu.VMEM((tt, d), F32)],
        compiler_params=_params("arbitrary", "arbitrary"),
        name="conv_dw_prompt",
    )(u, w_dw, b_dw.reshape(1, d), g_ln.reshape(1, d), b_ln.reshape(1, d))


def _conv_sample_kernel(hist_ref, u_ref, w_ref, bdw_ref, g_ref, b_ref, o_ref):
    nb = hist_ref.shape[0]
    nh = CONV_WIDTH - 1
    rows = []
    for bi in range(nb):
        y = jnp.sum(hist_ref[bi] * w_ref[0:nh, :], axis=0, keepdims=True)
        rows.append(y + u_ref[bi:bi + 1, :] * w_ref[nh:nh + 1, :] + bdw_ref[...])
    y = jnp.concatenate(rows, axis=0)
    z = _ln_silu(y, g_ref[...], b_ref[...])
    o_ref[...] = jnp.concatenate([z, jnp.zeros((o_ref.shape[0] - nb, z.shape[1]), F32)], axis=0).astype(o_ref.dtype)


def _conv_sample(hist, u, w_dw, b_dw, g_ln, b_ln):
    d = u.shape[1]
    return pl.pallas_call(
        _conv_sample_kernel,
        out_shape=jax.ShapeDtypeStruct((SAMPLE_ROWS, d), BF16),
        compiler_params=pltpu.CompilerParams(vmem_limit_bytes=VMEM_LIMIT),
        name="conv_dw_sample",
    )(hist, u, w_dw, b_dw.reshape(1, d), g_ln.reshape(1, d), b_ln.reshape(1, d))


def kernel(x_prompt, x_sample, cache_k, cache_v, cache_logf, state_conv, page_table, c_prompt, c_sample,
           w_mod, b_mod, g_mix, g_ffn, w_qkvf, b_f, g_q, g_k, w_o,
           w_pw1, b_pw1, w_dw, b_dw, g_cln, b_cln, w_pw2, b_pw2, w_ff1, w_ff2):
    nbp, t, d = x_prompt.shape
    nbs = x_sample.shape[0]
    depth = w_mod.shape[0]
    n_fox = cache_k.shape[0]
    mp = nbp * t
    pool, page = cache_k.shape[1], cache_k.shape[2]
    assert x_sample.shape[1] == 1 and nbp + nbs <= SAMPLE_ROWS
    assert LANES % N_HEADS == 0 and (page * N_HEADS) % LANES == 0

    c16 = jnp.zeros((SAMPLE_ROWS, d), F32).at[:nbp].set(c_prompt).at[nbp:nbp + nbs].set(c_sample)
    mods = _mods(c16, w_mod, b_mod)

    xp = x_prompt.reshape(mp, d)
    xs = jnp.zeros((SAMPLE_ROWS, d), F32).at[:nbs].set(x_sample.reshape(nbs, d))

    wf_pad = jnp.zeros((n_fox, d, LANES), F32).at[:, :, :N_HEADS].set(w_qkvf[:, :, 3 * d:])
    bf_pad = jnp.zeros((n_fox, 1, LANES), F32).at[:, 0, :N_HEADS].set(b_f)
    cache_lf = cache_logf.reshape(n_fox, pool, page * N_HEADS // LANES, LANES)

    kp_l, vp_l, lfp_l, cvp_l, ks_l, vs_l, lfs_l, cvs_l = [], [], [], [], [], [], [], []
    tm_p, tm_s = 1024, SAMPLE_ROWS
    for i in range(depth):
        j = i // 2
        modp = mods[i, :nbp].reshape(nbp, 1, 6 * d)
        mods_s = jnp.zeros((1, SAMPLE_ROWS, 6 * d), F32).at[0, :nbs].set(mods[i, nbp:nbp + nbs])
        if i % 2 == 0:
            q, k32, kb, v32, vb, lf = _qkvf(xp, g_mix[i], modp, t, w_qkvf, j, wf_pad[j], bf_pad[j], g_q[j], g_k[j],
                                            tm_p, 256)
            lf_p = lf[:, :N_HEADS].reshape(nbp, t, N_HEADS)
            c = _cumsum_rows(jnp.transpose(lf_p, (0, 2, 1)).reshape(nbp * N_HEADS, t))
            o = _flash(q.reshape(nbp, t, d), kb.reshape(nbp, t, d), vb.reshape(nbp, t, d),
                       c.reshape(nbp, N_HEADS, 1, t))
            xp = _mm_res(o.reshape(mp, d), w_o, j, None, xp, modp, 2, t, tm_p, 512, 2048)
            kp_l.append(k32.reshape(nbp, t, N_HEADS, HEAD_DIM))
            vp_l.append(v32.reshape(nbp, t, N_HEADS, HEAD_DIM))
            lfp_l.append(lf_p)
            q, k32, kb, v32, vb, lf = _qkvf(xs, g_mix[i], mods_s, SAMPLE_ROWS, w_qkvf, j, wf_pad[j], bf_pad[j],
                                            g_q[j], g_k[j], tm_s, 512)
            lf_s = lf[:nbs, :N_HEADS]
            o = _decode_attn(page_table, j,
                             q[:nbs].reshape(nbs, N_HEADS, HEAD_DIM),
                             k32[:nbs].reshape(nbs, N_HEADS, HEAD_DIM),
                             v32[:nbs].reshape(nbs, N_HEADS, HEAD_DIM),
                             jnp.tile(lf_s, (1, LANES // N_HEADS)).reshape(nbs, 1, LANES),
                             cache_k, cache_v, cache_lf)
            o16 = jnp.zeros((SAMPLE_ROWS, d), BF16).at[:nbs].set(o.reshape(nbs, d).astype(BF16))
            xs = _mm_res(o16, w_o, j, None, xs, mods_s, 2, SAMPLE_ROWS, tm_s, 1024, d)
            ks_l.append(k32[:nbs].reshape(nbs, 1, N_HEADS, HEAD_DIM))
            vs_l.append(v32[:nbs].reshape(nbs, 1, N_HEADS, HEAD_DIM))
            lfs_l.append(lf_s.reshape(nbs, 1, N_HEADS))
        else:
            u = _pw1(xp, g_mix[i], modp, t, w_pw1, b_pw1, j, tm_p, 512)
            u3 = u.reshape(nbp, t, d)
            act = _conv_prompt(u3, w_dw[j], b_dw[j], g_cln[j], b_cln[j])
            xp = _mm_res(act.reshape(mp, d), w_pw2, j, b_pw2[j], xp, modp, 2, t, tm_p, 512, 2048)
            cvp_l.append(u3[:, t - (CONV_WIDTH - 1):])
            u = _pw1(xs, g_mix[i], mods_s, SAMPLE_ROWS, w_pw1, b_pw1, j, tm_s, 1024)
            act = _conv_sample(state_conv[j], u, w_dw[j], b_dw[j], g_cln[j], b_cln[j])
            xs = _mm_res(act, w_pw2, j, b_pw2[j], xs, mods_s, 2, SAMPLE_ROWS, tm_s, 1024, d)
            cvs_l.append(jnp.concatenate([state_conv[j][:, 1:], u[:nbs, None, :]], axis=1))
        hid = _ff1(xp, g_ffn[i], modp, t, w_ff1, i, tm_p, 512)
        xp = _mm_res(hid, w_ff2, i, None, xp, modp, 5, t, tm_p, 512, 2048)
        hid = _ff1(xs, g_ffn[i], mods_s, SAMPLE_ROWS, w_ff1, i, tm_s, 1024)
        xs = _mm_res(hid, w_ff2, i, None, xs, mods_s, 5, SAMPLE_ROWS, tm_s, 1024, 2048)

    return (xp.reshape(nbp, t, d), xs[:nbs].reshape(nbs, 1, d),
            jnp.stack(kp_l), jnp.stack(vp_l), jnp.stack(lfp_l), jnp.stack(cvp_l),
            jnp.stack(ks_l), jnp.stack(vs_l), jnp.stack(lfs_l), jnp.stack(cvs_l))
```

```python
import functools
import math

import jax
import jax.numpy as jnp
from jax import lax
from jax.experimental import pallas as pl
from jax.experimental.pallas import tpu as pltpu

F32 = jnp.float32
BF16 = jnp.bfloat16

N_HEADS = 16
HEAD_DIM = 128
CONV_WIDTH = 31
EPS = 1e-6
NEG_INF = -1e30
LOG2E = math.log2(math.e)
LANES = 128
SUBLANES = 8
VMEM_LIMIT = 56 * 1024 * 1024
SAMPLE_ROWS = 16
CONV_HALO = 32
CONV_ROWS = 64
PAGES_PER_STEP = 8


def _params(*sem):
    return pltpu.CompilerParams(dimension_semantics=sem, vmem_limit_bytes=VMEM_LIMIT)


def _dot(a, b):
    return jnp.dot(a, b, preferred_element_type=F32)


def _dot_nt(a, b):
    return lax.dot_general(a, b, (((1,), (1,)), ((), ())), preferred_element_type=F32)


def _split3(x):
    hi = x.astype(BF16)
    r1 = x - hi.astype(F32)
    mid = r1.astype(BF16)
    lo = (r1 - mid.astype(F32)).astype(BF16)
    return hi, mid, lo


def _dot3(x, m):
    hi, mid, lo = _split3(x)
    return _dot(hi, m) + _dot(mid, m) + _dot(lo, m)


def _modnorm(x, g, shift, scale):
    ms = jnp.mean(x * x, axis=-1, keepdims=True)
    y = x * lax.rsqrt(ms + EPS) * g
    return y * (1.0 + scale) + shift


def _log_sigmoid(z):
    return jnp.minimum(z, 0.0) - jnp.log1p(jnp.exp(-jnp.abs(z)))


def _mod_spec(mod, k, d, tiles_per_group):
    r = mod.shape[1]
    return pl.BlockSpec((None, r, d), lambda i, *_: (i // tiles_per_group, 0, k))


def _mods_kernel(c_ref, w_ref, b_ref, o_ref):
    c = c_ref[...]
    a = (c * jax.nn.sigmoid(c)).astype(BF16)
    o_ref[...] = _dot(a, w_ref[...].astype(BF16)) + b_ref[...]


def _mods(c16, w_mod, b_mod, tn=1024):
    depth, d, n = w_mod.shape
    return pl.pallas_call(
        _mods_kernel,
        out_shape=jax.ShapeDtypeStruct((depth, SAMPLE_ROWS, n), F32),
        grid=(depth, n // tn),
        in_specs=[
            pl.BlockSpec((SAMPLE_ROWS, d), lambda l, j: (0, 0)),
            pl.BlockSpec((None, d, tn), lambda l, j: (l, 0, j)),
            pl.BlockSpec((None, 1, tn), lambda l, j: (l, 0, j)),
        ],
        out_specs=pl.BlockSpec((None, SAMPLE_ROWS, tn), lambda l, j: (l, 0, j)),
        compiler_params=_params("arbitrary", "arbitrary"),
        name="adaln_mods",
    )(c16, w_mod, b_mod.reshape(depth, 1, n))


def _qkvf_kernel(x_ref, g_ref, sh_ref, sc_ref, wq_ref, wk_ref, wv_ref, wf_ref, bf_ref, gq_ref, gk_ref,
                 q_ref, k32_ref, kb_ref, v32_ref, vb_ref, lf_ref, h_ref, *, tn):
    @pl.when(pl.program_id(1) == 0)
    def _():
        hb = _modnorm(x_ref[...], g_ref[...], sh_ref[...], sc_ref[...]).astype(BF16)
        h_ref[...] = hb
        lf_ref[...] = _log_sigmoid(_dot(hb, wf_ref[...].astype(BF16)) + bf_ref[...])

    hb = h_ref[...]
    zq = _dot(hb, wq_ref[...].astype(BF16))
    zk = _dot(hb, wk_ref[...].astype(BF16))
    zv = _dot(hb, wv_ref[...].astype(BF16))
    q_scale = HEAD_DIM ** -0.5 * LOG2E
    for hh in range(tn // HEAD_DIM):
        sl = slice(hh * HEAD_DIM, (hh + 1) * HEAD_DIM)
        zh = zq[:, sl]
        qn = zh * lax.rsqrt(jnp.mean(zh * zh, axis=-1, keepdims=True) + EPS) * gq_ref[...]
        q_ref[:, sl] = (qn * q_scale).astype(BF16)
        zh = zk[:, sl]
        kn = zh * lax.rsqrt(jnp.mean(zh * zh, axis=-1, keepdims=True) + EPS) * gk_ref[...]
        k32_ref[:, sl] = kn
        kb_ref[:, sl] = kn.astype(BF16)
    v32_ref[...] = zv
    vb_ref[...] = zv.astype(BF16)


def _qkvf(x, g, mod, rows_per_group, w_all, layer, wf_pad, bf_pad, g_q, g_k, tm, tn):
    m, d = x.shape
    nb = d // tn
    tpg = rows_per_group // tm
    row = lambda i, j: (i, j)
    vec = lambda i, j: (0, 0)
    return pl.pallas_call(
        functools.partial(_qkvf_kernel, tn=tn),
        out_shape=(
            jax.ShapeDtypeStruct((m, d), BF16),
            jax.ShapeDtypeStruct((m, d), F32),
            jax.ShapeDtypeStruct((m, d), BF16),
            jax.ShapeDtypeStruct((m, d), F32),
            jax.ShapeDtypeStruct((m, d), BF16),
            jax.ShapeDtypeStruct((m, LANES), F32),
        ),
        grid=(m // tm, nb),
        in_specs=[
            pl.BlockSpec((tm, d), lambda i, j: (i, 0)),
            pl.BlockSpec((1, d), vec),
            _mod_spec(mod, 0, d, tpg),
            _mod_spec(mod, 1, d, tpg),
            pl.BlockSpec((None, d, tn), lambda i, j: (layer, 0, j)),
            pl.BlockSpec((None, d, tn), lambda i, j: (layer, 0, nb + j)),
            pl.BlockSpec((None, d, tn), lambda i, j: (layer, 0, 2 * nb + j)),
            pl.BlockSpec((d, LANES), vec),
            pl.BlockSpec((1, LANES), vec),
            pl.BlockSpec((1, HEAD_DIM), vec),
            pl.BlockSpec((1, HEAD_DIM), vec),
        ],
        out_specs=(
            pl.BlockSpec((tm, tn), row), pl.BlockSpec((tm, tn), row), pl.BlockSpec((tm, tn), row),
            pl.BlockSpec((tm, tn), row), pl.BlockSpec((tm, tn), row),
            pl.BlockSpec((tm, LANES), lambda i, j: (i, 0)),
        ),
        scratch_shapes=[pltpu.VMEM((tm, d), BF16)],
        compiler_params=_params("arbitrary", "arbitrary"),
        name="fox_qkvf",
    )(x, g.reshape(1, d), mod, mod, w_all, w_all, w_all, wf_pad, bf_pad, g_q.reshape(1, -1), g_k.reshape(1, -1))


def _cumsum_kernel(x_ref, o_ref):
    n = x_ref.shape[1]
    j = lax.broadcasted_iota(jnp.int32, (LANES, LANES), 0)
    s = lax.broadcasted_iota(jnp.int32, (LANES, LANES), 1)
    tri = (j <= s).astype(BF16)
    carry = jnp.zeros((x_ref.shape[0], 1), F32)
    for c in range(n // LANES):
        sl = slice(c * LANES, (c + 1) * LANES)
        cs = _dot3(x_ref[:, sl], tri)
        o_ref[:, sl] = cs + carry
        carry = carry + cs[:, LANES - 1:LANES]


def _cumsum_rows(x):
    return pl.pallas_call(
        _cumsum_kernel,
        out_shape=jax.ShapeDtypeStruct(x.shape, F32),
        name="logf_cumsum",
    )(x)


def _flash_kernel(q_ref, k_ref, v_ref, c_ref, o_ref, kaug_ref, caq_ref, vt_ref, acc_ref,
                  sa_ref, sb_ref, pa_ref, pb_ref, *, tq):
    i = pl.program_id(2)
    t = k_ref.shape[0]

    @pl.when(i == 0)
    def _():
        lane = lax.broadcasted_iota(jnp.int32, (LANES, LANES), 1)

        def build(ci, carry):
            r0 = pl.multiple_of(ci * LANES, LANES)
            crow = c_ref[:, pl.ds(r0, LANES)] * LOG2E
            col = jnp.broadcast_to(crow, (LANES, LANES)).T
            hi, mid, lo = [p.astype(F32) for p in _split3(col)]
            pieces_k = jnp.where(lane == 0, hi, jnp.where(lane == 1, mid, jnp.where(lane == 2, lo, 0.0)))
            ones_k = jnp.where((lane >= 3) & (lane < 6), 1.0, 0.0)
            pieces_q = jnp.where(lane == 3, hi, jnp.where(lane == 4, mid, jnp.where(lane == 5, lo, 0.0)))
            ones_q = jnp.where(lane < 3, -1.0, 0.0)
            kaug_ref[pl.ds(r0, LANES), 0:HEAD_DIM] = k_ref[pl.ds(r0, LANES), :]
            kaug_ref[pl.ds(r0, LANES), HEAD_DIM:2 * HEAD_DIM] = (pieces_k + ones_k).astype(BF16)
            caq_ref[pl.ds(r0, LANES), :] = (pieces_q + ones_q).astype(BF16)
            vt_ref[:, pl.ds(r0, LANES)] = v_ref[pl.ds(r0, LANES), :].astype(F32).T.astype(BF16)
            return carry

        lax.fori_loop(0, t // LANES, build, 0)

    q0 = pl.multiple_of(i * tq, tq)
    q_aug = jnp.concatenate([q_ref[...], caq_ref[pl.ds(q0, tq), :]], axis=1)
    tk = tq // 2
    acc_ref[...] = jnp.zeros_like(acc_ref)
    pb_ref[...] = jnp.zeros_like(pb_ref)

    def qk(k0):
        return _dot_nt(kaug_ref[pl.ds(k0, tk), :], q_aug)

    def soft(s, m, l, mask_from):
        if mask_from is not None:
            kk = lax.broadcasted_iota(jnp.int32, (tk, tq), 0) + mask_from
            qq = lax.broadcasted_iota(jnp.int32, (tk, tq), 1)
            s = jnp.where(kk <= qq, s, NEG_INF)
        m_new = jnp.maximum(m, jnp.max(s, axis=0, keepdims=True))
        alpha = jnp.exp2(m - m_new)
        p = jnp.exp2(s - m_new)
        return m_new, alpha * l + jnp.sum(p, axis=0, keepdims=True), alpha, p.astype(BF16)

    def pv(alpha, k0, p):
        acc_ref[...] = alpha * acc_ref[...] + _dot(vt_ref[:, pl.ds(k0, tk)], p)

    def pair(jj, carry):
        m, l, alpha = carry
        ka = pl.multiple_of(jj * tq, tq)
        sb_ref[...] = qk(ka + tk)
        pv(alpha, pl.multiple_of(jnp.maximum(ka - tk, 0), tk), pb_ref[...])
        m, l, alpha, p = soft(sa_ref[...], m, l, None)
        pa_ref[...] = p
        sa_ref[...] = qk(ka + tq)
        pv(alpha, ka, pa_ref[...])
        m, l, alpha, p = soft(sb_ref[...], m, l, None)
        pb_ref[...] = p
        return m, l, alpha

    sa_ref[...] = qk(0)
    m0 = jnp.full((1, tq), NEG_INF, F32)
    l0 = jnp.zeros((1, tq), F32)
    m, l, alpha = lax.fori_loop(0, i, pair, (m0, l0, jnp.ones((1, tq), F32)))
    sb_ref[...] = qk(q0 + tk)
    pv(alpha, pl.multiple_of(jnp.maximum(q0 - tk, 0), tk), pb_ref[...])
    m, l, alpha, p = soft(sa_ref[...], m, l, 0)
    pv(alpha, q0, p)
    m, l, alpha, p = soft(sb_ref[...], m, l, tk)
    pv(alpha, q0 + tk, p)
    ot = acc_ref[...] / l
    for c in range(tq // LANES):
        sl = slice(c * LANES, (c + 1) * LANES)
        o_ref[sl, :] = ot[:, sl].T.astype(o_ref.dtype)


def _flash(q, k, v, c, tq=1024):
    b, t, d = q.shape
    h = d // HEAD_DIM
    return pl.pallas_call(
        functools.partial(_flash_kernel, tq=tq),
        out_shape=jax.ShapeDtypeStruct((b, t, d), BF16),
        grid=(b, h, t // tq),
        in_specs=[
            pl.BlockSpec((None, tq, HEAD_DIM), lambda bi, hi, i: (bi, i, hi)),
            pl.BlockSpec((None, t, HEAD_DIM), lambda bi, hi, i: (bi, 0, hi)),
            pl.BlockSpec((None, t, HEAD_DIM), lambda bi, hi, i: (bi, 0, hi)),
            pl.BlockSpec((None, None, 1, t), lambda bi, hi, i: (bi, hi, 0, 0)),
        ],
        out_specs=pl.BlockSpec((None, tq, HEAD_DIM), lambda bi, hi, i: (bi, i, hi)),
        scratch_shapes=[
            pltpu.VMEM((t, 2 * HEAD_DIM), BF16),
            pltpu.VMEM((t, HEAD_DIM), BF16),
            pltpu.VMEM((HEAD_DIM, t), BF16),
            pltpu.VMEM((HEAD_DIM, tq), F32),
            pltpu.VMEM((tq // 2, tq), F32), pltpu.VMEM((tq // 2, tq), F32),
            pltpu.VMEM((tq // 2, tq), BF16), pltpu.VMEM((tq // 2, tq), BF16),
        ],
        compiler_params=_params("arbitrary", "arbitrary", "arbitrary"),
        name="fox_flash",
    )(q, k, v, c)


def _decode_kernel(pt_ref, q_ref, kn_ref, vn_ref, lfn_ref, *rest, n_grp):
    k_refs, v_refs, lf_refs = rest[:n_grp], rest[n_grp:2 * n_grp], rest[2 * n_grp:3 * n_grp]
    o_ref, acc_ref, m_ref, l_ref, carry_ref = rest[3 * n_grp:]
    p = pl.program_id(1)
    page = k_refs[0].shape[0]
    rows = page * N_HEADS
    grp = rows // LANES

    head_s = lax.broadcasted_iota(jnp.int32, (N_HEADS, LANES), 0)
    lane_h = lax.broadcasted_iota(jnp.int32, (N_HEADS, LANES), 1) % N_HEADS
    own = head_s == lane_h

    def to_pattern(col):
        return jnp.sum(jnp.where(own, col, 0.0), axis=0, keepdims=True)

    def to_column(pat):
        first = lax.broadcasted_iota(jnp.int32, (N_HEADS, LANES), 1) == head_s
        return jnp.sum(jnp.where(first, pat, 0.0), axis=1, keepdims=True)

    @pl.when(p == 0)
    def _():
        kn = kn_ref[...].astype(BF16).astype(F32)
        s_new = jnp.sum(q_ref[...].astype(F32) * kn, axis=-1, keepdims=True)
        m_ref[...] = to_pattern(s_new)
        l_ref[...] = jnp.ones_like(l_ref)
        acc_ref[...] = vn_ref[...].astype(BF16).astype(F32)
        carry_ref[...] = lfn_ref[...] * LOG2E

    a = lax.broadcasted_iota(jnp.int32, (LANES, LANES), 0)
    b = lax.broadcasted_iota(jnp.int32, (LANES, LANES), 1)
    same_head = (a % N_HEADS) == (b % N_HEADS)
    newer_in_row = (same_head & (a // N_HEADS > b // N_HEADS)).astype(BF16)
    all_in_row = same_head.astype(BF16)
    qh = q_ref[...]

    scores, biases = [], []
    run = carry_ref[...]
    for g in range(n_grp):
        lf = lf_refs[g][...] * LOG2E
        row_tot = _dot3(lf, all_in_row)
        in_page = jnp.zeros((1, LANES), F32)
        newer = [None] * grp
        for r in reversed(range(grp)):
            newer[r] = in_page
            in_page = in_page + row_tot[r:r + 1, :]
        biases.append(jnp.concatenate(newer, axis=0) + _dot3(lf, newer_in_row) + run)
        run = run + in_page

        kb = k_refs[g][...].reshape(rows, HEAD_DIM).astype(BF16)
        st = _dot_nt(qh, kb)
        srows = []
        for r in range(grp):
            blk = st[:, r * LANES:(r + 1) * LANES]
            srows.append(jnp.sum(jnp.where(own, blk, 0.0), axis=0, keepdims=True))
        scores.append(jnp.concatenate(srows, axis=0))
    carry_ref[...] = run

    s = jnp.concatenate([scores[g] + biases[g] for g in range(n_grp)], axis=0)
    mx = jnp.max(s, axis=0, keepdims=True)
    for sh in (N_HEADS, 2 * N_HEADS, 4 * N_HEADS):
        mx = jnp.maximum(mx, pltpu.roll(mx, sh, axis=1))
    m_new = jnp.maximum(m_ref[...], mx)
    alpha = jnp.exp2(m_ref[...] - m_new)
    pr = jnp.exp2(s - m_new)
    ps = jnp.sum(pr, axis=0, keepdims=True)
    for sh in (N_HEADS, 2 * N_HEADS, 4 * N_HEADS):
        ps = ps + pltpu.roll(ps, sh, axis=1)
    l_ref[...] = alpha * l_ref[...] + ps
    m_ref[...] = m_new

    pv = jnp.zeros((N_HEADS, HEAD_DIM), F32)
    for g in range(n_grp):
        pcols = [jnp.where(own, jnp.broadcast_to(pr[g * grp + r:g * grp + r + 1, :], (N_HEADS, LANES)), 0.0)
                 for r in range(grp)]
        pmat = jnp.concatenate(pcols, axis=1).astype(BF16)
        pv = pv + _dot(pmat, v_refs[g][...].reshape(rows, HEAD_DIM).astype(BF16))
    acc_ref[...] = to_column(alpha) * acc_ref[...] + pv

    @pl.when(p == pl.num_programs(1) - 1)
    def _():
        o_ref[...] = acc_ref[...] / to_column(l_ref[...])


def _decode_attn(page_table, layer, q3, kn3, vn3, lfn_pat, cache_k, cache_v, cache_lf, n_grp=PAGES_PER_STEP):
    nb, n_pages = page_table.shape
    page = cache_k.shape[2]
    grp = page * N_HEADS // LANES
    last = n_pages - 1
    per_b = lambda b, p, pt: (b, 0, 0)

    def kv_spec(g):
        return pl.BlockSpec((None, None, page, N_HEADS, HEAD_DIM),
                            lambda b, p, pt: (layer, pt[b, last - (p * n_grp + g)], 0, 0, 0))

    def lf_spec(g):
        return pl.BlockSpec((None, None, grp, LANES),
                            lambda b, p, pt: (layer, pt[b, last - (p * n_grp + g)], 0, 0))

    return pl.pallas_call(
        functools.partial(_decode_kernel, n_grp=n_grp),
        out_shape=jax.ShapeDtypeStruct((nb, N_HEADS, HEAD_DIM), F32),
        grid_spec=pltpu.PrefetchScalarGridSpec(
            num_scalar_prefetch=1,
            grid=(nb, n_pages // n_grp),
            in_specs=[
                pl.BlockSpec((None, N_HEADS, HEAD_DIM), per_b),
                pl.BlockSpec((None, N_HEADS, HEAD_DIM), per_b),
                pl.BlockSpec((None, N_HEADS, HEAD_DIM), per_b),
                pl.BlockSpec((None, 1, LANES), per_b),
            ] + [kv_spec(g) for g in range(n_grp)] + [kv_spec(g) for g in range(n_grp)]
              + [lf_spec(g) for g in range(n_grp)],
            out_specs=pl.BlockSpec((None, N_HEADS, HEAD_DIM), per_b),
            scratch_shapes=[
                pltpu.VMEM((N_HEADS, HEAD_DIM), F32),
                pltpu.VMEM((1, LANES), F32),
                pltpu.VMEM((1, LANES), F32),
                pltpu.VMEM((1, LANES), F32),
            ],
        ),
        compiler_params=_params("arbitrary", "arbitrary"),
        name="fox_decode",
    )(page_table, q3, kn3, vn3, lfn_pat, *([cache_k] * n_grp), *([cache_v] * n_grp), *([cache_lf] * n_grp))


def _mm_res_kernel(*refs, has_bias):
    if has_bias:
        x_ref, w_ref, b_ref, res_ref, gate_ref, o_ref, acc_ref = refs
    else:
        x_ref, w_ref, res_ref, gate_ref, o_ref, acc_ref = refs
    kk = pl.program_id(2)

    @pl.when(kk == 0)
    def _():
        acc_ref[...] = jnp.zeros_like(acc_ref)

    acc_ref[...] += _dot(x_ref[...], w_ref[...].astype(BF16))

    @pl.when(kk == pl.num_programs(2) - 1)
    def _():
        acc = acc_ref[...]
        if has_bias:
            acc = acc + b_ref[...]
        o_ref[...] = res_ref[...] + gate_ref[...] * acc


def _mm_res(x, w_all, layer, b, res, mod, gate_k, rows_per_group, tm, tn, tk):
    m, k = x.shape
    n = w_all.shape[2]
    tpg = rows_per_group // tm
    ins = [x, w_all]
    specs = [pl.BlockSpec((tm, tk), lambda i, j, kk: (i, kk)),
             pl.BlockSpec((None, tk, tn), lambda i, j, kk: (layer, kk, j))]
    if b is not None:
        ins.append(b.reshape(1, n))
        specs.append(pl.BlockSpec((1, tn), lambda i, j, kk: (0, j)))
    ins += [res, mod]
    r = mod.shape[1]
    nblk = n // tn
    specs += [pl.BlockSpec((tm, tn), lambda i, j, kk: (i, j)),
              pl.BlockSpec((None, r, tn), lambda i, j, kk: (i // tpg, 0, gate_k * nblk + j))]
    return pl.pallas_call(
        functools.partial(_mm_res_kernel, has_bias=b is not None),
        out_shape=jax.ShapeDtypeStruct((m, n), F32),
        grid=(m // tm, n // tn, k // tk),
        in_specs=specs,
        out_specs=pl.BlockSpec((tm, tn), lambda i, j, kk: (i, j)),
        scratch_shapes=[pltpu.VMEM((tm, tn), F32)],
        compiler_params=_params("arbitrary", "arbitrary", "arbitrary"),
        name="matmul_gated_residual",
    )(*ins)


def _ff1_kernel(x_ref, g_ref, sh_ref, sc_ref, w_ref, o_ref, h_ref):
    @pl.when(pl.program_id(1) == 0)
    def _():
        h_ref[...] = _modnorm(x_ref[...], g_ref[...], sh_ref[...], sc_ref[...]).astype(BF16)

    z = jnp.maximum(_dot(h_ref[...], w_ref[...].astype(BF16)), 0.0)
    o_ref[...] = (z * z).astype(o_ref.dtype)


def _ff1(x, g, mod, rows_per_group, w_all, layer, tm, tn):
    m, d = x.shape
    n = w_all.shape[2]
    tpg = rows_per_group // tm
    return pl.pallas_call(
        _ff1_kernel,
        out_shape=jax.ShapeDtypeStruct((m, n), BF16),
        grid=(m // tm, n // tn),
        in_specs=[
            pl.BlockSpec((tm, d), lambda i, j: (i, 0)),
            pl.BlockSpec((1, d), lambda i, j: (0, 0)),
            _mod_spec(mod, 3, d, tpg),
            _mod_spec(mod, 4, d, tpg),
            pl.BlockSpec((None, d, tn), lambda i, j: (layer, 0, j)),
        ],
        out_specs=pl.BlockSpec((tm, tn), lambda i, j: (i, j)),
        scratch_shapes=[pltpu.VMEM((tm, d), BF16)],
        compiler_params=_params("arbitrary", "arbitrary"),
        name="mlp_up",
    )(x, g.reshape(1, d), mod, mod, w_all)


def _pw1_kernel(x_ref, g_ref, sh_ref, sc_ref, wa_ref, wg_ref, ba_ref, bg_ref, o_ref, h_ref):
    @pl.when(pl.program_id(1) == 0)
    def _():
        h_ref[...] = _modnorm(x_ref[...], g_ref[...], sh_ref[...], sc_ref[...]).astype(BF16)

    hb = h_ref[...]
    a = _dot(hb, wa_ref[...].astype(BF16)) + ba_ref[...]
    gt = _dot(hb, wg_ref[...].astype(BF16)) + bg_ref[...]
    o_ref[...] = a * jax.nn.sigmoid(gt)


def _pw1(x, g, mod, rows_per_group, w_all, b_all, layer, tm, tn):
    m, d = x.shape
    nb = d // tn
    tpg = rows_per_group // tm
    b3 = b_all.reshape(b_all.shape[0], 1, 2 * d)
    return pl.pallas_call(
        _pw1_kernel,
        out_shape=jax.ShapeDtypeStruct((m, d), F32),
        grid=(m // tm, nb),
        in_specs=[
            pl.BlockSpec((tm, d), lambda i, j: (i, 0)),
            pl.BlockSpec((1, d), lambda i, j: (0, 0)),
            _mod_spec(mod, 0, d, tpg),
            _mod_spec(mod, 1, d, tpg),
            pl.BlockSpec((None, d, tn), lambda i, j: (layer, 0, j)),
            pl.BlockSpec((None, d, tn), lambda i, j: (layer, 0, nb + j)),
            pl.BlockSpec((None, 1, tn), lambda i, j: (layer, 0, j)),
            pl.BlockSpec((None, 1, tn), lambda i, j: (layer, 0, nb + j)),
        ],
        out_specs=pl.BlockSpec((tm, tn), lambda i, j: (i, j)),
        scratch_shapes=[pltpu.VMEM((tm, d), BF16)],
        compiler_params=_params("arbitrary", "arbitrary"),
        name="conv_pw1_glu",
    )(x, g.reshape(1, d), mod, mod, w_all, w_all, b3, b3)


def _ln_silu(y, g, b):
    mu = jnp.mean(y, axis=-1, keepdims=True)
    yc = y - mu
    z = yc * lax.rsqrt(jnp.mean(yc * yc, axis=-1, keepdims=True) + EPS) * g + b
    return z * jax.nn.sigmoid(z)


def _conv_kernel(u_ref, w_ref, bdw_ref, g_ref, b_ref, o_ref, win_ref, y_ref, *, tt):
    d = win_ref.shape[1]

    @pl.when(pl.program_id(1) == 0)
    def _():
        win_ref[0:CONV_HALO, :] = jnp.zeros((CONV_HALO, d), F32)

    win_ref[CONV_HALO:CONV_HALO + tt, :] = u_ref[...]
    win_ref[CONV_HALO + tt:, :] = jnp.zeros((SUBLANES, d), F32)
    first = CONV_HALO - (CONV_WIDTH - 1)
    span = CONV_ROWS + SUBLANES

    def chunk(idx, carry):
        r0 = pl.multiple_of((idx // (d // LANES)) * CONV_ROWS, CONV_ROWS)
        c0 = pl.multiple_of((idx % (d // LANES)) * LANES, LANES)
        acc = jnp.zeros((CONV_ROWS, LANES), F32)
        for r in range(SUBLANES):
            z = None
            for k in range(CONV_WIDTH):
                off = first + k
                if off % SUBLANES != r:
                    continue
                term = win_ref[pl.ds(r0 + (off - r), span), pl.ds(c0, LANES)] * w_ref[k:k + 1, pl.ds(c0, LANES)]
                z = term if z is None else z + term
            if r:
                z = pltpu.roll(z, span - r, axis=0)
            acc = acc + z[:CONV_ROWS]
        y_ref[pl.ds(r0, CONV_ROWS), pl.ds(c0, LANES)] = acc
        return carry

    lax.fori_loop(0, (tt // CONV_ROWS) * (d // LANES), chunk, 0)

    nr = 4 * SUBLANES

    def norm(ci, carry):
        r0 = pl.multiple_of(ci * nr, nr)
        y = y_ref[pl.ds(r0, nr), :] + bdw_ref[...]
        o_ref[pl.ds(r0, nr), :] = _ln_silu(y, g_ref[...], b_ref[...]).astype(o_ref.dtype)
        return carry

    lax.fori_loop(0, tt // nr, norm, 0)
    win_ref[0:CONV_HALO, :] = win_ref[tt:tt + CONV_HALO, :]


def _conv_prompt(u, w_dw, b_dw, g_ln, b_ln, tt=512):
    b, t, d = u.shape
    vec = lambda bi, i: (0, 0)
    return pl.pallas_call(
        functools.partial(_conv_kernel, tt=tt),
        out_shape=jax.ShapeDtypeStruct((b, t, d), BF16),
        grid=(b, t // tt),
        in_specs=[
            pl.BlockSpec((None, tt, d), lambda bi, i: (bi, i, 0)),
            pl.BlockSpec((CONV_WIDTH, d), vec),
            pl.BlockSpec((1, d), vec),
            pl.BlockSpec((1, d), vec),
            pl.BlockSpec((1, d), vec),
        ],
        out_specs=pl.BlockSpec((None, tt, d), lambda bi, i: (bi, i, 0)),
        scratch_shapes=[pltpu.VMEM((CONV_HALO + tt + SUBLANES, d), F32), pltpu.VMEM((tt, d), F32)],
        compiler_params=_params("arbitrary", "arbitrary"),
        name="conv_dw_prompt",
    )(u, w_dw, b_dw.reshape(1, d), g_ln.reshape(1, d), b_ln.reshape(1, d))


def _conv_sample_kernel(hist_ref, u_ref, w_ref, bdw_ref, g_ref, b_ref, o_ref):
    nb = hist_ref.shape[0]
    nh = CONV_WIDTH - 1
    rows = []
    for bi in range(nb):
        y = jnp.sum(hist_ref[bi] * w_ref[0:nh, :], axis=0, keepdims=True)
        rows.append(y + u_ref[bi:bi + 1, :] * w_ref[nh:nh + 1, :] + bdw_ref[...])
    y = jnp.concatenate(rows, axis=0)
    z = _ln_silu(y, g_ref[...], b_ref[...])
    o_ref[...] = jnp.concatenate([z, jnp.zeros((o_ref.shape[0] - nb, z.shape[1]), F32)], axis=0).astype(o_ref.dtype)


def _conv_sample(hist, u, w_dw, b_dw, g_ln, b_ln):
    d = u.shape[1]
    return pl.pallas_call(
        _conv_sample_kernel,
        out_shape=jax.ShapeDtypeStruct((SAMPLE_ROWS, d), BF16),
        compiler_params=pltpu.CompilerParams(vmem_limit_bytes=VMEM_LIMIT),
        name="conv_dw_sample",
    )(hist, u, w_dw, b_dw.reshape(1, d), g_ln.reshape(1, d), b_ln.reshape(1, d))


def kernel(x_prompt, x_sample, cache_k, cache_v, cache_logf, state_conv, page_table, c_prompt, c_sample,
           w_mod, b_mod, g_mix, g_ffn, w_qkvf, b_f, g_q, g_k, w_o,
           w_pw1, b_pw1, w_dw, b_dw, g_cln, b_cln, w_pw2, b_pw2, w_ff1, w_ff2):
    nbp, t, d = x_prompt.shape
    nbs = x_sample.shape[0]
    depth = w_mod.shape[0]
    n_fox = cache_k.shape[0]
    mp = nbp * t
    pool, page = cache_k.shape[1], cache_k.shape[2]
    assert x_sample.shape[1] == 1 and nbp + nbs <= SAMPLE_ROWS
    assert LANES % N_HEADS == 0 and (page * N_HEADS) % LANES == 0

    c16 = jnp.zeros((SAMPLE_ROWS, d), F32).at[:nbp].set(c_prompt).at[nbp:nbp + nbs].set(c_sample)
    mods = _mods(c16, w_mod, b_mod)

    xp = x_prompt.reshape(mp, d)
    xs = jnp.zeros((SAMPLE_ROWS, d), F32).at[:nbs].set(x_sample.reshape(nbs, d))

    wf_pad = jnp.zeros((n_fox, d, LANES), F32).at[:, :, :N_HEADS].set(w_qkvf[:, :, 3 * d:])
    bf_pad = jnp.zeros((n_fox, 1, LANES), F32).at[:, 0, :N_HEADS].set(b_f)
    cache_lf = cache_logf.reshape(n_fox, pool, page * N_HEADS // LANES, LANES)

    kp_l, vp_l, lfp_l, cvp_l, ks_l, vs_l, lfs_l, cvs_l = [], [], [], [], [], [], [], []
    tm_p, tm_s = 1024, SAMPLE_ROWS
    for i in range(depth):
        j = i // 2
        modp = mods[i, :nbp].reshape(nbp, 1, 6 * d)
        mods_s = jnp.zeros((1, SAMPLE_ROWS, 6 * d), F32).at[0, :nbs].set(mods[i, nbp:nbp + nbs])
        if i % 2 == 0:
            q, k32, kb, v32, vb, lf = _qkvf(xp, g_mix[i], modp, t, w_qkvf, j, wf_pad[j], bf_pad[j], g_q[j], g_k[j],
                                            tm_p, 256)
            lf_p = lf[:, :N_HEADS].reshape(nbp, t, N_HEADS)
            c = _cumsum_rows(jnp.transpose(lf_p, (0, 2, 1)).reshape(nbp * N_HEADS, t))
            o = _flash(q.reshape(nbp, t, d), kb.reshape(nbp, t, d), vb.reshape(nbp, t, d),
                       c.reshape(nbp, N_HEADS, 1, t))
            xp = _mm_res(o.reshape(mp, d), w_o, j, None, xp, modp, 2, t, tm_p, 1024, 1024)
            kp_l.append(k32.reshape(nbp, t, N_HEADS, HEAD_DIM))
            vp_l.append(v32.reshape(nbp, t, N_HEADS, HEAD_DIM))
            lfp_l.append(lf_p)
            q, k32, kb, v32, vb, lf = _qkvf(xs, g_mix[i], mods_s, SAMPLE_ROWS, w_qkvf, j, wf_pad[j], bf_pad[j],
                                            g_q[j], g_k[j], tm_s, 512)
            lf_s = lf[:nbs, :N_HEADS]
            o = _decode_attn(page_table, j,
                             q[:nbs].reshape(nbs, N_HEADS, HEAD_DIM),
                             k32[:nbs].reshape(nbs, N_HEADS, HEAD_DIM),
                             v32[:nbs].reshape(nbs, N_HEADS, HEAD_DIM),
                             jnp.tile(lf_s, (1, LANES // N_HEADS)).reshape(nbs, 1, LANES),
                             cache_k, cache_v, cache_lf)
            o16 = jnp.zeros((SAMPLE_ROWS, d), BF16).at[:nbs].set(o.reshape(nbs, d).astype(BF16))
            xs = _mm_res(o16, w_o, j, None, xs, mods_s, 2, SAMPLE_ROWS, tm_s, 1024, d)
            ks_l.append(k32[:nbs].reshape(nbs, 1, N_HEADS, HEAD_DIM))
            vs_l.append(v32[:nbs].reshape(nbs, 1, N_HEADS, HEAD_DIM))
            lfs_l.append(lf_s.reshape(nbs, 1, N_HEADS))
        else:
            u = _pw1(xp, g_mix[i], modp, t, w_pw1, b_pw1, j, tm_p, 512)
            u3 = u.reshape(nbp, t, d)
            act = _conv_prompt(u3, w_dw[j], b_dw[j], g_cln[j], b_cln[j])
            xp = _mm_res(act.reshape(mp, d), w_pw2, j, b_pw2[j], xp, modp, 2, t, tm_p, 1024, 1024)
            cvp_l.append(u3[:, t - (CONV_WIDTH - 1):])
            u = _pw1(xs, g_mix[i], mods_s, SAMPLE_ROWS, w_pw1, b_pw1, j, tm_s, 1024)
            act = _conv_sample(state_conv[j], u, w_dw[j], b_dw[j], g_cln[j], b_cln[j])
            xs = _mm_res(act, w_pw2, j, b_pw2[j], xs, mods_s, 2, SAMPLE_ROWS, tm_s, 1024, d)
            cvs_l.append(jnp.concatenate([state_conv[j][:, 1:], u[:nbs, None, :]], axis=1))
        hid = _ff1(xp, g_ffn[i], modp, t, w_ff1, i, tm_p, 512)
        xp = _mm_res(hid, w_ff2, i, None, xp, modp, 5, t, tm_p, 1024, 1024)
        hid = _ff1(xs, g_ffn[i], mods_s, SAMPLE_ROWS, w_ff1, i, tm_s, 1024)
        xs = _mm_res(hid, w_ff2, i, None, xs, mods_s, 5, SAMPLE_ROWS, tm_s, 1024, 2048)

    return (xp.reshape(nbp, t, d), xs[:nbs].reshape(nbs, 1, d),
            jnp.stack(kp_l), jnp.stack(vp_l), jnp.stack(lfp_l), jnp.stack(cvp_l),
            jnp.stack(ks_l), jnp.stack(vs_l), jnp.stack(lfs_l), jnp.stack(cvs_l))
```
